```python
import jax, jax.numpy as jnp
from jax import lax
import numpy as np

D_MODEL = 1024
BATCH = 8
SEQ = 2048
DEPTH = 1

LRU_WIDTH = 1024
LRU_BLOCKS = 16
LRU_BLOCK_W = LRU_WIDTH // LRU_BLOCKS
CONV_WIDTH = 4
LRU_C = 8.0
HEAD_DIM = 64
N_Q_HEADS = 16
N_KV_HEADS = 4
Q_PER_KV = N_Q_HEADS // N_KV_HEADS
ATTN_WIDTH = N_Q_HEADS * HEAD_DIM
KV_WIDTH = N_KV_HEADS * HEAD_DIM
WINDOW = 128
BLOCK = 128
ROPE_THETA = 10000.0
D_FF = 2816
MACARON_SCALE = 0.5
NORM_EPS = 1e-6
MASK_VALUE = -1e30
IN_WIDTH = 2 * LRU_WIDTH + ATTN_WIDTH + 2 * KV_WIDTH + 2 * D_MODEL
IN_SPLITS = (
    LRU_WIDTH,
    2 * LRU_WIDTH,
    2 * LRU_WIDTH + ATTN_WIDTH,
    2 * LRU_WIDTH + ATTN_WIDTH + KV_WIDTH,
    2 * LRU_WIDTH + ATTN_WIDTH + 2 * KV_WIDTH,
    2 * LRU_WIDTH + ATTN_WIDTH + 2 * KV_WIDTH + D_MODEL,
)

kernel_name = "hybrid_rglru_swa_sink_macaron_block"


def rms_norm(x, g):
    xf = x.astype(jnp.float32)
    y = xf * lax.rsqrt(jnp.mean(xf * xf, axis=-1, keepdims=True) + NORM_EPS)
    return (y * g.astype(jnp.float32)).astype(x.dtype)


def swiglu(x, w_gu, w_down):
    g, u = jnp.split(x @ w_gu, 2, axis=-1)
    return (jax.nn.silu(g) * u) @ w_down


def rope_tables(seq_len):
    half = HEAD_DIM // 2
    inv_freq = ROPE_THETA ** (-jnp.arange(half, dtype=jnp.float32) / half)
    ang = jnp.arange(seq_len, dtype=jnp.float32)[:, None] * inv_freq[None, :]
    return jnp.cos(ang), jnp.sin(ang)


def apply_rope(x, cos, sin):
    half = HEAD_DIM // 2
    xf = x.astype(jnp.float32)
    x1, x2 = xf[..., :half], xf[..., half:]
    c = cos[None, :, None, :]
    s = sin[None, :, None, :]
    return jnp.concatenate([x1 * c - x2 * s, x2 * c + x1 * s], axis=-1).astype(x.dtype)


def causal_depthwise_conv(x, w, b):
    s = x.shape[1]
    xp = jnp.pad(x, ((0, 0), (CONV_WIDTH - 1, 0), (0, 0)))
    y = xp[:, 0:s] * w[0]
    for k in range(1, CONV_WIDTH):
        y = y + xp[:, k:k + s] * w[k]
    return y + b


def block_diag_linear(x, w, b):
    bsz, s, _ = x.shape
    xb = x.reshape(bsz, s, LRU_BLOCKS, LRU_BLOCK_W)
    y = jnp.einsum('bsnc,ncd->bsnd', xb, w).reshape(bsz, s, LRU_WIDTH)
    return y + b


def rg_lru(x, w_a, b_a, w_x, b_x, lam):
    xf = x.astype(jnp.float32)
    r = jax.nn.sigmoid(block_diag_linear(x, w_a, b_a).astype(jnp.float32))
    i = jax.nn.sigmoid(block_diag_linear(x, w_x, b_x).astype(jnp.float32))
    log_a = -LRU_C * r * jax.nn.softplus(-lam.astype(jnp.float32))
    a = jnp.exp(log_a)
    mult = jnp.sqrt(jnp.maximum(-jnp.expm1(2.0 * log_a), 0.0))
    bterm = mult * (i * xf)

    def combine(left, right):
        a1, b1 = left
        a2, b2 = right
        return a1 * a2, a2 * b1 + b2

    _, h = lax.associative_scan(combine, (a, bterm), axis=1)
    return h.astype(x.dtype)


def band_blocks(t):
    bsz, s, h, d = t.shape
    tb = t.reshape(bsz, s // BLOCK, BLOCK, h, d)
    prev = jnp.pad(tb, ((0, 0), (1, 0), (0, 0), (0, 0), (0, 0)))[:, :-1]
    return jnp.concatenate([prev, tb], axis=2)


def sliding_window_attention_with_sinks(q, k, v, sinks):
    bsz, s = q.shape[0], q.shape[1]
    nb = s // BLOCK
    qb = q.reshape(bsz, nb, BLOCK, N_KV_HEADS, Q_PER_KV, HEAD_DIM)
    kb = band_blocks(k)
    vb = band_blocks(v)
    scores = jnp.einsum('bnqhgd,bnshd->bhgnqs', qb, kb).astype(jnp.float32) * (HEAD_DIM ** -0.5)
    qi = jnp.arange(BLOCK)[:, None]
    si = jnp.arange(2 * BLOCK)[None, :]
    diff = BLOCK + qi - si
    key_pos = (jnp.arange(nb)[:, None, None] - 1) * BLOCK + si[None]
    mask = ((diff >= 0) & (diff < WINDOW))[None] & (key_pos >= 0)
    scores = jnp.where(mask, scores, MASK_VALUE)
    sink = sinks.astype(jnp.float32).reshape(1, N_KV_HEADS, Q_PER_KV, 1, 1, 1)
    sink = jnp.broadcast_to(sink, scores.shape[:-1] + (1,))
    probs = jax.nn.softmax(jnp.concatenate([scores, sink], axis=-1), axis=-1)[..., :-1]
    out = jnp.einsum('bhgnqs,bnshd->bnqhgd', probs.astype(v.dtype), vb)
    return out.reshape(bsz, s, ATTN_WIDTH)


def hybrid_mixer(u, cos, sin, w_in, conv_w, conv_b, lru_w_a, lru_b_a, lru_w_x, lru_b_x,
                 lru_lambda, attn_sinks, w_proj_lru, w_proj_attn, w_out):
    bsz, s, _ = u.shape
    z = u @ w_in
    gate_br, x_br, q, k, v, g_lru, g_attn = jnp.split(z, IN_SPLITS, axis=-1)
    xc = causal_depthwise_conv(x_br, conv_w, conv_b)
    y_lru = rg_lru(xc, lru_w_a, lru_b_a, lru_w_x, lru_b_x, lru_lambda) * jax.nn.gelu(gate_br)
    q = apply_rope(q.reshape(bsz, s, N_Q_HEADS, HEAD_DIM), cos, sin)
    k = apply_rope(k.reshape(bsz, s, N_KV_HEADS, HEAD_DIM), cos, sin)
    v = v.reshape(bsz, s, N_KV_HEADS, HEAD_DIM)
    y_attn = sliding_window_attention_with_sinks(q, k, v, attn_sinks)
    merged = jax.nn.sigmoid(g_lru) * (y_lru @ w_proj_lru) + jax.nn.sigmoid(g_attn) * (y_attn @ w_proj_attn)
    return merged @ w_out


def setup_inputs(seed: int = 0) -> dict:
    key = jax.random.key(seed)
    ks = jax.random.split(key, 24)

    def nrm(k, shape, scale):
        return jax.random.normal(k, shape, jnp.float32) * scale

    def gain(k, shape):
        return 1.0 + 0.05 * jax.random.normal(k, shape, jnp.float32)

    u = jax.random.uniform(ks[13], (DEPTH, LRU_WIDTH), jnp.float32, minval=0.9, maxval=0.999)
    a0 = u ** (1.0 / LRU_C)
    lam = jnp.log(a0) - jnp.log1p(-a0)
    return {
        "x": nrm(ks[0], (BATCH, SEQ, D_MODEL), 1.0),
        "ffn1_pre_g": gain(ks[1], (DEPTH, D_MODEL)),
        "ffn1_w_gu": nrm(ks[2], (DEPTH, D_MODEL, 2 * D_FF), D_MODEL ** -0.5),
        "ffn1_w_down": nrm(ks[3], (DEPTH, D_FF, D_MODEL), D_FF ** -0.5),
        "ffn1_post_g": gain(ks[4], (DEPTH, D_MODEL)),
        "mix_pre_g": gain(ks[5], (DEPTH, D_MODEL)),
        "w_in": nrm(ks[6], (DEPTH, D_MODEL, IN_WIDTH), D_MODEL ** -0.5),
        "conv_w": nrm(ks[7], (DEPTH, CONV_WIDTH, LRU_WIDTH), CONV_WIDTH ** -0.5),
        "conv_b": nrm(ks[8], (DEPTH, LRU_WIDTH), 0.01),
        "lru_w_a": nrm(ks[9], (DEPTH, LRU_BLOCKS, LRU_BLOCK_W, LRU_BLOCK_W), LRU_BLOCK_W ** -0.5),
        "lru_b_a": nrm(ks[10], (DEPTH, LRU_WIDTH), 0.01),
        "lru_w_x": nrm(ks[11], (DEPTH, LRU_BLOCKS, LRU_BLOCK_W, LRU_BLOCK_W), LRU_BLOCK_W ** -0.5),
        "lru_b_x": nrm(ks[12], (DEPTH, LRU_WIDTH), 0.01),
        "lru_lambda": lam,
        "attn_sinks": nrm(ks[14], (DEPTH, N_Q_HEADS), 0.5),
        "w_proj_lru": nrm(ks[15], (DEPTH, LRU_WIDTH, D_MODEL), LRU_WIDTH ** -0.5),
        "w_proj_attn": nrm(ks[16], (DEPTH, ATTN_WIDTH, D_MODEL), ATTN_WIDTH ** -0.5),
        "w_out": nrm(ks[17], (DEPTH, D_MODEL, D_MODEL), D_MODEL ** -0.5),
        "mix_post_g": gain(ks[18], (DEPTH, D_MODEL)),
        "ffn2_pre_g": gain(ks[19], (DEPTH, D_MODEL)),
        "ffn2_w_gu": nrm(ks[20], (DEPTH, D_MODEL, 2 * D_FF), D_MODEL ** -0.5),
        "ffn2_w_down": nrm(ks[21], (DEPTH, D_FF, D_MODEL), D_FF ** -0.5),
        "ffn2_post_g": gain(ks[22], (DEPTH, D_MODEL)),
    }


def reference(x, ffn1_pre_g, ffn1_w_gu, ffn1_w_down, ffn1_post_g,
              mix_pre_g, w_in, conv_w, conv_b, lru_w_a, lru_b_a, lru_w_x, lru_b_x,
              lru_lambda, attn_sinks, w_proj_lru, w_proj_attn, w_out, mix_post_g,
              ffn2_pre_g, ffn2_w_gu, ffn2_w_down, ffn2_post_g):
    cos, sin = rope_tables(x.shape[1])
    h = x
    for l in range(DEPTH):
        f = swiglu(rms_norm(h, ffn1_pre_g[l]), ffn1_w_gu[l], ffn1_w_down[l])
        h = h + MACARON_SCALE * rms_norm(f, ffn1_post_g[l])
        m = hybrid_mixer(rms_norm(h, mix_pre_g[l]), cos, sin, w_in[l], conv_w[l], conv_b[l],
                         lru_w_a[l], lru_b_a[l], lru_w_x[l], lru_b_x[l], lru_lambda[l],
                         attn_sinks[l], w_proj_lru[l], w_proj_attn[l], w_out[l])
        h = h + rms_norm(m, mix_post_g[l])
        f = swiglu(rms_norm(h, ffn2_pre_g[l]), ffn2_w_gu[l], ffn2_w_down[l])
        h = h + MACARON_SCALE * rms_norm(f, ffn2_post_g[l])
    return h
```

```python
import functools

import jax
import jax.numpy as jnp
import numpy as np
from jax import lax
from jax.experimental import pallas as pl
from jax.experimental.pallas import tpu as pltpu

D_MODEL = 1024
LRU_WIDTH = 1024
LRU_BLOCKS = 16
LRU_BLOCK_W = LRU_WIDTH // LRU_BLOCKS
CONV_WIDTH = 4
LRU_C = 8.0
HEAD_DIM = 64
HALF = HEAD_DIM // 2
N_Q_HEADS = 16
N_KV_HEADS = 4
ATTN_WIDTH = N_Q_HEADS * HEAD_DIM
KV_WIDTH = N_KV_HEADS * HEAD_DIM
WINDOW = 128
BLOCK = 128
ROPE_THETA = 10000.0
D_FF = 2816
MACARON_SCALE = 0.5
NORM_EPS = 1e-6
MASK_VALUE = -1e30

LANES = 128
SUBLANES = 8
MXU_DIM = 256
VMEM_LIMIT_BYTES = 56 * 1024 * 1024

FFN_TOKENS = 512
FFN_CHUNK = 256
MIX_TOKENS = 256

C_GATE = 0
C_X = C_GATE + LRU_WIDTH
C_Q = C_X + LRU_WIDTH
C_K = C_Q + ATTN_WIDTH
C_V = C_K + 2 * KV_WIDTH
C_GL = C_V + 2 * KV_WIDTH
C_GA = C_GL + D_MODEL
IN_WIDTH_P = C_GA + D_MODEL


def _rms_norm(x, g):
    return x * lax.rsqrt(jnp.mean(x * x, axis=-1, keepdims=True) + NORM_EPS) * g


def _const_spec(shape):
    n = len(shape)
    return pl.BlockSpec(shape, lambda *_: (0,) * n, pipeline_mode=pl.Buffered(1))


def _ffn_kernel(h_ref, pre_g_ref, w_gu_ref, w_down_ref, post_g_ref, o_ref, act_ref):
    h = h_ref[...]
    xn = _rms_norm(h, pre_g_ref[...]).astype(jnp.bfloat16)
    for c in range(D_FF // FFN_CHUNK):
        lo = c * FFN_CHUNK
        g = jnp.dot(xn, w_gu_ref[:, lo:lo + FFN_CHUNK], preferred_element_type=jnp.float32)
        u = jnp.dot(xn, w_gu_ref[:, D_FF + lo:D_FF + lo + FFN_CHUNK],
                    preferred_element_type=jnp.float32)
        act_ref[:, lo:lo + FFN_CHUNK] = (jax.nn.silu(g) * u).astype(jnp.bfloat16)
    f = jnp.dot(act_ref[...], w_down_ref[...], preferred_element_type=jnp.float32)
    o_ref[...] = h + MACARON_SCALE * _rms_norm(f, post_g_ref[...])


def _ffn(h2d, pre_g, w_gu, w_down, post_g):
    m = h2d.shape[0]
    return pl.pallas_call(
        _ffn_kernel,
        name="ffn",
        grid=(m // FFN_TOKENS,),
        in_specs=[
            pl.BlockSpec((FFN_TOKENS, D_MODEL), lambda i: (i, 0)),
            _const_spec((1, D_MODEL)),
            _const_spec((D_MODEL, 2 * D_FF)),
            _const_spec((D_FF, D_MODEL)),
            _const_spec((1, D_MODEL)),
        ],
        out_specs=pl.BlockSpec((FFN_TOKENS, D_MODEL), lambda i: (i, 0)),
        out_shape=jax.ShapeDtypeStruct((m, D_MODEL), jnp.float32),
        scratch_shapes=[pltpu.VMEM((FFN_TOKENS, D_FF), jnp.bfloat16)],
        compiler_params=pltpu.CompilerParams(
            dimension_semantics=("arbitrary",), vmem_limit_bytes=VMEM_LIMIT_BYTES),
    )(h2d, pre_g, w_gu, w_down, post_g)


def _mixer_kernel(sinks_ref, h_ref, cos_ref, sin_ref, pre_g_ref, w_in_ref, conv_w_ref,
                  conv_b_ref, w_gate_ref, b_a_ref, b_x_ref, lam_ref, w_pl_ref, w_pa_ref,
                  w_out_ref, post_g_ref, o_ref,
                  xbuf_ref, hstate_ref, kprev_ref, vprev_ref, yattn_ref):
    ts = MIX_TOKENS
    t_idx = pl.program_id(1)

    @pl.when(t_idx == 0)
    def _():
        xbuf_ref[0:SUBLANES, :] = jnp.zeros((SUBLANES, LRU_WIDTH), jnp.float32)
        hstate_ref[...] = jnp.zeros_like(hstate_ref)
        kprev_ref[...] = jnp.zeros_like(kprev_ref)
        vprev_ref[...] = jnp.zeros_like(vprev_ref)

    h = h_ref[...]
    un = _rms_norm(h, pre_g_ref[...]).astype(jnp.bfloat16)

    def proj(lo, width):
        return jnp.dot(un, w_in_ref[:, lo:lo + width], preferred_element_type=jnp.float32)

    xbuf_ref[SUBLANES:SUBLANES + ts, :] = proj(C_X, LRU_WIDTH)
    xc = conv_b_ref[...]
    for k in range(CONV_WIDTH):
        off = SUBLANES - (CONV_WIDTH - 1) + k
        xc = xc + xbuf_ref[off:off + ts, :] * conv_w_ref[k:k + 1, :]
    xbuf_ref[0:SUBLANES, :] = xbuf_ref[ts:ts + SUBLANES, :]

    xc_b = xc.astype(jnp.bfloat16)
    pre_r, pre_i = [], []
    for gidx in range(LRU_WIDTH // MXU_DIM):
        lo = gidx * MXU_DIM
        ri = jnp.dot(xc_b[:, lo:lo + MXU_DIM], w_gate_ref[gidx],
                     preferred_element_type=jnp.float32)
        pre_r.append(ri[:, :MXU_DIM])
        pre_i.append(ri[:, MXU_DIM:])
    r = jax.nn.sigmoid(jnp.concatenate(pre_r, axis=1) + b_a_ref[...])
    i_gate = jax.nn.sigmoid(jnp.concatenate(pre_i, axis=1) + b_x_ref[...])
    lam = lam_ref[...]
    softplus_neg_lam = jnp.maximum(-lam, 0.0) + jnp.log1p(jnp.exp(-jnp.abs(lam)))
    log_a = (-LRU_C) * r * softplus_neg_lam
    a = jnp.exp(log_a)
    mult = jnp.sqrt(jnp.maximum(-jnp.tanh(log_a) * (a * a + 1.0), 0.0))
    b = mult * (i_gate * xc)

    row_in_group = lax.broadcasted_iota(jnp.int32, (ts, LRU_WIDTH), 0) % SUBLANES
    shift = 1
    while shift < SUBLANES:
        keep = row_in_group >= shift
        a_prev = jnp.where(keep, pltpu.roll(a, shift, axis=0), 1.0)
        b_prev = jnp.where(keep, pltpu.roll(b, shift, axis=0), 0.0)
        b = a * b_prev + b
        a = a * a_prev
        shift *= 2
    carry = hstate_ref[...]
    groups = []
    for gidx in range(ts // SUBLANES):
        lo = gidx * SUBLANES
        hg = a[lo:lo + SUBLANES, :] * carry + b[lo:lo + SUBLANES, :]
        groups.append(hg)
        carry = hg[SUBLANES - 1:SUBLANES, :]
    hstate_ref[...] = carry
    h_lru = jnp.concatenate(groups, axis=0)
    y_lru = (h_lru * jax.nn.gelu(proj(C_GATE, LRU_WIDTH))).astype(jnp.bfloat16)
    p_lru = jnp.dot(y_lru, w_pl_ref[...], preferred_element_type=jnp.float32)

    cos = cos_ref[...]
    sin = sin_ref[...]
    scale = HEAD_DIM ** -0.5

    def rope(x, c, s):
        return x * c + pltpu.roll(x, LANES // 2, axis=1) * s

    q = proj(C_Q, ATTN_WIDTH)
    q_pairs = [rope(q[:, p * LANES:(p + 1) * LANES], cos * scale, sin * scale)
               .astype(jnp.bfloat16) for p in range(N_Q_HEADS // 2)]
    k = proj(C_K, 2 * KV_WIDTH)
    k_rot = jnp.concatenate(
        [rope(k[:, j * LANES:(j + 1) * LANES], cos, sin) for j in range(N_KV_HEADS)],
        axis=1).astype(jnp.bfloat16)
    v = proj(C_V, 2 * KV_WIDTH).astype(jnp.bfloat16)
    k_all = jnp.concatenate([kprev_ref[...].astype(jnp.bfloat16), k_rot], axis=0)
    v_all = jnp.concatenate([vprev_ref[...].astype(jnp.bfloat16), v], axis=0)
    kprev_ref[...] = k_rot[ts - BLOCK:, :].astype(jnp.float32)
    vprev_ref[...] = v[ts - BLOCK:, :].astype(jnp.float32)

    lane = lax.broadcasted_iota(jnp.int32, (BLOCK + ts, LANES), 1)
    first_of_pair_k = (lane % HEAD_DIM) < HALF
    first_of_pair_v = lane < HEAD_DIM
    zero_b = jnp.zeros((), jnp.bfloat16)

    qi = lax.broadcasted_iota(jnp.int32, (BLOCK, 2 * BLOCK), 0)
    si = lax.broadcasted_iota(jnp.int32, (BLOCK, 2 * BLOCK), 1)
    diff = BLOCK + qi - si
    band = (diff >= 0) & (diff < WINDOW)
    out_lane = lax.broadcasted_iota(jnp.int32, (BLOCK, LANES), 1)

    for j in range(N_KV_HEADS):
        kj = k_all[:, j * LANES:(j + 1) * LANES]
        vj = v_all[:, j * LANES:(j + 1) * LANES]
        k_first = jnp.where(first_of_pair_k, kj, zero_b)
        k_second = jnp.where(first_of_pair_k, zero_b, kj)
        v_first = jnp.where(first_of_pair_v, vj, zero_b)
        v_second = jnp.where(first_of_pair_v, zero_b, vj)
        for qb in range(ts // BLOCK):
            r0 = qb * BLOCK
            mask = band & ((si >= BLOCK) | (t_idx * (ts // BLOCK) + qb > 0))
            for pp in range(2):
                p = 2 * j + pp
                qp = q_pairs[p][r0:r0 + BLOCK, :]
                halves = []
                for (kk, vv, head) in ((k_first, v_first, 2 * p), (k_second, v_second, 2 * p + 1)):
                    s = lax.dot_general(qp, kk[r0:r0 + 2 * BLOCK, :], (((1,), (1,)), ((), ())),
                                        preferred_element_type=jnp.float32)
                    s = jnp.where(mask, s, MASK_VALUE)
                    sink = sinks_ref[head]
                    m = jnp.maximum(jnp.max(s, axis=1, keepdims=True), sink)
                    e = jnp.exp(s - m)
                    denom = jnp.sum(e, axis=1, keepdims=True) + jnp.exp(sink - m)
                    pv = jnp.dot(e.astype(jnp.bfloat16), vv[r0:r0 + 2 * BLOCK, :],
                                 preferred_element_type=jnp.float32)
                    halves.append((pv, 1.0 / denom))
                (pv0, inv0), (pv1, inv1) = halves
                out = (pv0 + pv1) * jnp.where(out_lane < HEAD_DIM, inv0, inv1)
                yattn_ref[r0:r0 + BLOCK, p * LANES:(p + 1) * LANES] = out.astype(jnp.bfloat16)

    p_attn = jnp.dot(yattn_ref[...], w_pa_ref[...], preferred_element_type=jnp.float32)

    merged = (jax.nn.sigmoid(proj(C_GL, D_MODEL)) * p_lru
              + jax.nn.sigmoid(proj(C_GA, D_MODEL)) * p_attn)
    m_out = jnp.dot(merged.astype(jnp.bfloat16), w_out_ref[...],
                    preferred_element_type=jnp.float32)
    o_ref[...] = h + _rms_norm(m_out, post_g_ref[...])


def _mixer(h3d, sinks, cos_t, sin_t, pre_g, w_in_p, conv_w, conv_b, w_gate, b_a, b_x, lam,
           w_pl, w_pa, w_out, post_g):
    bsz, s, _ = h3d.shape
    ts = MIX_TOKENS
    tile = pl.BlockSpec((None, ts, D_MODEL), lambda b, t: (b, t, 0))
    rope_spec = pl.BlockSpec((ts, LANES), lambda b, t: (t, 0))
    return pl.pallas_call(
        _mixer_kernel,
        name="mixer",
        grid=(bsz, s // ts),
        in_specs=[
            pl.BlockSpec(memory_space=pltpu.SMEM),
            tile, rope_spec, rope_spec,
            _const_spec((1, D_MODEL)),
            _const_spec((D_MODEL, IN_WIDTH_P)),
            _const_spec((CONV_WIDTH, LRU_WIDTH)),
            _const_spec((1, LRU_WIDTH)),
            _const_spec((LRU_WIDTH // MXU_DIM, MXU_DIM, 2 * MXU_DIM)),
            _const_spec((1, LRU_WIDTH)),
            _const_spec((1, LRU_WIDTH)),
            _const_spec((1, LRU_WIDTH)),
            _const_spec((LRU_WIDTH, D_MODEL)),
            _const_spec((ATTN_WIDTH, D_MODEL)),
            _const_spec((D_MODEL, D_MODEL)),
            _const_spec((1, D_MODEL)),
        ],
        out_specs=tile,
        out_shape=jax.ShapeDtypeStruct(h3d.shape, jnp.float32),
        scratch_shapes=[
            pltpu.VMEM((SUBLANES + ts, LRU_WIDTH), jnp.float32),
            pltpu.VMEM((1, LRU_WIDTH), jnp.float32),
            pltpu.VMEM((BLOCK, 2 * KV_WIDTH), jnp.float32),
            pltpu.VMEM((BLOCK, 2 * KV_WIDTH), jnp.float32),
            pltpu.VMEM((ts, ATTN_WIDTH), jnp.bfloat16),
        ],
        compiler_params=pltpu.CompilerParams(
            dimension_semantics=("arbitrary", "arbitrary"), vmem_limit_bytes=VMEM_LIMIT_BYTES),
    )(sinks, h3d, cos_t, sin_t, pre_g, w_in_p, conv_w, conv_b, w_gate, b_a, b_x, lam,
      w_pl, w_pa, w_out, post_g)


def _relayout_w_in(w_in):
    d = w_in.shape[0]
    o = 2 * LRU_WIDTH
    wq = w_in[:, o:o + ATTN_WIDTH].reshape(d, N_Q_HEADS // 2, 2, 2, HALF)
    wq = wq.transpose(0, 1, 3, 2, 4).reshape(d, ATTN_WIDTH)
    o += ATTN_WIDTH
    wk = w_in[:, o:o + KV_WIDTH].reshape(d, N_KV_HEADS, 2, 1, HALF)
    wk = jnp.broadcast_to(wk, (d, N_KV_HEADS, 2, 2, HALF)).reshape(d, 2 * KV_WIDTH)
    o += KV_WIDTH
    wv = w_in[:, o:o + KV_WIDTH].reshape(d, N_KV_HEADS, 1, HEAD_DIM)
    wv = jnp.broadcast_to(wv, (d, N_KV_HEADS, 2, HEAD_DIM)).reshape(d, 2 * KV_WIDTH)
    o += KV_WIDTH
    return jnp.concatenate(
        [w_in[:, :2 * LRU_WIDTH], wq, wk, wv, w_in[:, o:]], axis=1).astype(jnp.bfloat16)


def _gate_weights(w_a, w_x):
    per = MXU_DIM // LRU_BLOCK_W
    eye = jnp.eye(per, dtype=w_a.dtype)

    def expand(w):
        w4 = w.reshape(LRU_BLOCKS // per, per, LRU_BLOCK_W, LRU_BLOCK_W)
        return jnp.einsum("gncd,nm->gncmd", w4, eye).reshape(LRU_BLOCKS // per, MXU_DIM, MXU_DIM)

    return jnp.concatenate([expand(w_a), expand(w_x)], axis=2).astype(jnp.bfloat16)


def _rope_tables(seq_len):
    inv_freq = ROPE_THETA ** (-jnp.arange(HALF, dtype=jnp.float32) / HALF)
    ang = jnp.arange(seq_len, dtype=jnp.float32)[:, None] * inv_freq[None, :]
    cos, sin = jnp.cos(ang), jnp.sin(ang)
    return (jnp.concatenate([cos, cos, cos, cos], axis=1),
            jnp.concatenate([-sin, -sin, sin, sin], axis=1))


def kernel(x, ffn1_pre_g, ffn1_w_gu, ffn1_w_down, ffn1_post_g, mix_pre_g, w_in, conv_w, conv_b, lru_w_a, lru_b_a, lru_w_x, lru_b_x, lru_lambda, attn_sinks, w_proj_lru, w_proj_attn, w_out, mix_post_g, ffn2_pre_g, ffn2_w_gu, ffn2_w_down, ffn2_post_g):
    bsz, s, d = x.shape
    depth = ffn1_pre_g.shape[0]
    cos_t, sin_t = _rope_tables(s)
    bf = jnp.bfloat16
    h = x
    for l in range(depth):
        h = _ffn(h.reshape(bsz * s, d), ffn1_pre_g[l][None], ffn1_w_gu[l].astype(bf),
                 ffn1_w_down[l].astype(bf), ffn1_post_g[l][None]).reshape(bsz, s, d)
        h = _mixer(h, attn_sinks[l], cos_t, sin_t, mix_pre_g[l][None], _relayout_w_in(w_in[l]),
                   conv_w[l], conv_b[l][None], _gate_weights(lru_w_a[l], lru_w_x[l]),
                   lru_b_a[l][None], lru_b_x[l][None], lru_lambda[l][None],
                   w_proj_lru[l].astype(bf), w_proj_attn[l].astype(bf), w_out[l].astype(bf),
                   mix_post_g[l][None])
        h = _ffn(h.reshape(bsz * s, d), ffn2_pre_g[l][None], ffn2_w_gu[l].astype(bf),
                 ffn2_w_down[l].astype(bf), ffn2_post_g[l][None]).reshape(bsz, s, d)
    return h
```

```python
import jax
import jax.numpy as jnp
from jax import lax
from jax.experimental import pallas as pl
from jax.experimental.pallas import tpu as pltpu

D_MODEL = 1024
LRU_WIDTH = 1024
LRU_BLOCKS = 16
LRU_BLOCK_W = LRU_WIDTH // LRU_BLOCKS
CONV_WIDTH = 4
LRU_C = 8.0
HEAD_DIM = 64
HALF = HEAD_DIM // 2
N_Q_HEADS = 16
N_KV_HEADS = 4
ATTN_WIDTH = N_Q_HEADS * HEAD_DIM
KV_WIDTH = N_KV_HEADS * HEAD_DIM
WINDOW = 128
BLOCK = 128
ROPE_THETA = 10000.0
D_FF = 2816
MACARON_SCALE = 0.5
NORM_EPS = 1e-6
MASK_VALUE = -1e30

LANES = 128
SUBLANES = 8
MXU_DIM = 256
VMEM_LIMIT_BYTES = 56 * 1024 * 1024

FFN_TOKENS = 512
FFN_CHUNK = 256
MIX_TOKENS = 256

GROUPS = LRU_WIDTH // MXU_DIM
K_X, K_GATE, K_Q, K_KV, K_GL, K_GA = range(6)
GROUP_COLS = 6 * MXU_DIM
IN_WIDTH_P = GROUPS * GROUP_COLS
LOG2E = 1.4426950408889634
PAD_COLS = LANES


def _rms_norm(x, g):
    return x * lax.rsqrt(jnp.mean(x * x, axis=-1, keepdims=True) + NORM_EPS) * g


def _const_spec(shape):
    n = len(shape)
    return pl.BlockSpec(shape, lambda *_: (0,) * n, pipeline_mode=pl.Buffered(1))


def _ffn_kernel(h_ref, pre_g_ref, w_gu_ref, w_down_ref, post_g_ref, o_ref, act_ref):
    h = h_ref[...]
    xn = _rms_norm(h, pre_g_ref[...]).astype(jnp.bfloat16)
    for c in range(D_FF // FFN_CHUNK):
        lo = c * FFN_CHUNK
        g = jnp.dot(xn, w_gu_ref[:, lo:lo + FFN_CHUNK], preferred_element_type=jnp.float32)
        u = jnp.dot(xn, w_gu_ref[:, D_FF + lo:D_FF + lo + FFN_CHUNK],
                    preferred_element_type=jnp.float32)
        act_ref[:, lo:lo + FFN_CHUNK] = (jax.nn.silu(g) * u).astype(jnp.bfloat16)
    f = [jnp.dot(act_ref[...], w_down_ref[:, n * MXU_DIM:(n + 1) * MXU_DIM],
                 preferred_element_type=jnp.float32) for n in range(D_MODEL // MXU_DIM)]
    mean_sq = sum(jnp.sum(fn * fn, axis=1, keepdims=True) for fn in f) * (1.0 / D_MODEL)
    scale = MACARON_SCALE * lax.rsqrt(mean_sq + NORM_EPS)
    for n in range(D_MODEL // MXU_DIM):
        cn = slice(n * MXU_DIM, (n + 1) * MXU_DIM)
        o_ref[:, cn] = h_ref[:, cn] + f[n] * scale * post_g_ref[:, cn]


def _ffn(h2d, pre_g, w_gu, w_down, post_g):
    m = h2d.shape[0]
    return pl.pallas_call(
        _ffn_kernel,
        name="ffn",
        grid=(m // FFN_TOKENS,),
        in_specs=[
            pl.BlockSpec((FFN_TOKENS, D_MODEL), lambda i: (i, 0)),
            _const_spec((1, D_MODEL)),
            _const_spec((D_MODEL, 2 * D_FF)),
            _const_spec((D_FF, D_MODEL + PAD_COLS)),
            _const_spec((1, D_MODEL)),
        ],
        out_specs=pl.BlockSpec((FFN_TOKENS, D_MODEL), lambda i: (i, 0)),
        out_shape=jax.ShapeDtypeStruct((m, D_MODEL), jnp.float32),
        scratch_shapes=[pltpu.VMEM((FFN_TOKENS, D_FF), jnp.bfloat16)],
        compiler_params=pltpu.CompilerParams(
            dimension_semantics=("arbitrary",), vmem_limit_bytes=VMEM_LIMIT_BYTES),
    )(h2d, pre_g, w_gu, w_down, post_g)


def _mixer_kernel(sinks_ref, h_ref, cos_ref, sin_ref, pre_g_ref, w_in_ref, conv_w_ref,
                  conv_b_ref, w_gate_ref, b_a_ref, b_x_ref, lam_ref, w_pl_ref, w_pa_ref,
                  w_out_ref, post_g_ref, o_ref,
                  un_ref, xbuf_ref, hstate_ref, q_ref, kall_ref, vall_ref, ylru_ref, yattn_ref,
                  sgl_ref, sga_ref, plru_ref, merged_ref):
    ts = MIX_TOKENS
    t_idx = pl.program_id(1)
    f32, bf16 = jnp.float32, jnp.bfloat16

    @pl.when(t_idx == 0)
    def _():
        xbuf_ref[0:SUBLANES, :] = jnp.zeros((SUBLANES, LRU_WIDTH), f32)
        hstate_ref[...] = jnp.zeros_like(hstate_ref)
        kall_ref[0:BLOCK, :] = jnp.zeros((BLOCK, 2 * KV_WIDTH), bf16)
        vall_ref[0:BLOCK, :] = jnp.zeros((BLOCK, 2 * KV_WIDTH), bf16)

    un_ref[...] = _rms_norm(h_ref[...], pre_g_ref[...]).astype(bf16)

    def proj(g, kind):
        lo = g * GROUP_COLS + kind * MXU_DIM
        return jnp.dot(un_ref[...], w_in_ref[:, lo:lo + MXU_DIM], preferred_element_type=f32)

    def rope(x, c, s):
        return x * c + pltpu.roll(x, LANES // 2, axis=1) * s

    cos = cos_ref[...]
    sin = sin_ref[...]
    q_scale = (HEAD_DIM ** -0.5) * LOG2E
    cos_q = cos * q_scale
    sin_q = sin * q_scale
    row_in_group = lax.broadcasted_iota(jnp.int32, (ts, MXU_DIM), 0) % SUBLANES

    for g in range(GROUPS):
        cg = slice(g * MXU_DIM, (g + 1) * MXU_DIM)

        qg = proj(g, K_Q)
        for pp in range(2):
            lo = g * MXU_DIM + pp * LANES
            q_ref[:, lo:lo + LANES] = rope(qg[:, pp * LANES:(pp + 1) * LANES], cos_q, sin_q).astype(bf16)
        kv = proj(g, K_KV)
        kall_ref[BLOCK:BLOCK + ts, g * LANES:(g + 1) * LANES] = rope(kv[:, :LANES], cos, sin).astype(bf16)
        vall_ref[BLOCK:BLOCK + ts, g * LANES:(g + 1) * LANES] = kv[:, LANES:].astype(bf16)

        xbuf_ref[SUBLANES:SUBLANES + ts, cg] = proj(g, K_X)
        xc = conv_b_ref[:, cg]
        for k in range(CONV_WIDTH):
            off = SUBLANES - (CONV_WIDTH - 1) + k
            xc = xc + xbuf_ref[off:off + ts, cg] * conv_w_ref[k:k + 1, cg]
        xbuf_ref[0:SUBLANES, cg] = xbuf_ref[ts:ts + SUBLANES, cg]
        xc_b = xc.astype(bf16)
        r = jax.nn.sigmoid(jnp.dot(xc_b, w_gate_ref[g, :, :MXU_DIM], preferred_element_type=f32)
                           + b_a_ref[:, cg])
        i_gate = jax.nn.sigmoid(jnp.dot(xc_b, w_gate_ref[g, :, MXU_DIM:], preferred_element_type=f32)
                                + b_x_ref[:, cg])
        lam = lam_ref[:, cg]
        softplus_neg_lam = jnp.maximum(-lam, 0.0) + jnp.log1p(jnp.exp(-jnp.abs(lam)))
        log_a = (-LRU_C) * r * softplus_neg_lam
        a = jnp.exp(log_a)
        mult = jnp.sqrt(jnp.maximum(-jnp.tanh(log_a) * (a * a + 1.0), 0.0))
        b = mult * (i_gate * xc)
        shift = 1
        while shift < SUBLANES:
            keep = row_in_group >= shift
            a_prev = jnp.where(keep, pltpu.roll(a, shift, axis=0), 1.0)
            b_prev = jnp.where(keep, pltpu.roll(b, shift, axis=0), 0.0)
            b = a * b_prev + b
            a = a * a_prev
            shift *= 2
        carry = hstate_ref[:, cg]
        rows = []
        for i in range(ts // SUBLANES):
            lo = i * SUBLANES
            hg = a[lo:lo + SUBLANES, :] * carry + b[lo:lo + SUBLANES, :]
            rows.append(hg)
            carry = hg[SUBLANES - 1:SUBLANES, :]
        hstate_ref[:, cg] = carry
        h_lru = jnp.concatenate(rows, axis=0)
        ylru_ref[:, cg] = (h_lru * jax.nn.gelu(proj(g, K_GATE))).astype(bf16)

        sgl_ref[:, cg] = jax.nn.sigmoid(proj(g, K_GL))
        sga_ref[:, cg] = jax.nn.sigmoid(proj(g, K_GA))

    lane = lax.broadcasted_iota(jnp.int32, (2 * BLOCK, LANES), 1)
    first_of_pair_k = (lane % HEAD_DIM) < HALF
    first_of_pair_v = lane < HEAD_DIM
    zero_b = jnp.zeros((), bf16)
    qi = lax.broadcasted_iota(jnp.int32, (2 * BLOCK, 2 * BLOCK), 0) % BLOCK
    si = lax.broadcasted_iota(jnp.int32, (2 * BLOCK, 2 * BLOCK), 1)
    diff = BLOCK + qi - si
    band = (diff >= 0) & (diff < WINDOW)
    first_pair_rows = lax.broadcasted_iota(jnp.int32, (2 * BLOCK, 1), 0) < BLOCK

    for j in range(N_KV_HEADS):
        for qb in range(ts // BLOCK):
            r0 = qb * BLOCK
            q2 = jnp.concatenate(
                [q_ref[r0:r0 + BLOCK, (2 * j + pp) * LANES:(2 * j + pp + 1) * LANES] for pp in range(2)],
                axis=0)
            kk = kall_ref[r0:r0 + 2 * BLOCK, j * LANES:(j + 1) * LANES]
            vv = vall_ref[r0:r0 + 2 * BLOCK, j * LANES:(j + 1) * LANES]
            mask = band & ((si >= BLOCK) | (t_idx * (ts // BLOCK) + qb > 0))
            parts = []
            for second in (False, True):
                k_half = jnp.where(first_of_pair_k != second, kk, zero_b)
                v_half = jnp.where(first_of_pair_v != second, vv, zero_b)
                head = 4 * j + int(second)
                sink = jnp.where(first_pair_rows, sinks_ref[head] * LOG2E, sinks_ref[head + 2] * LOG2E)
                s = lax.dot_general(q2, k_half, (((1,), (1,)), ((), ())), preferred_element_type=f32)
                s = jnp.where(mask, s, MASK_VALUE)
                m = jnp.maximum(jnp.max(s, axis=1, keepdims=True), sink)
                e = jnp.exp2(s - m)
                denom = jnp.sum(e, axis=1, keepdims=True) + jnp.exp2(sink - m)
                parts.append((jnp.dot(e.astype(bf16), v_half, preferred_element_type=f32), 1.0 / denom))
            (pv0, inv0), (pv1, inv1) = parts
            out = ((pv0 + pv1) * jnp.where(first_of_pair_v, inv0, inv1)).astype(bf16)
            for pp in range(2):
                lo = (2 * j + pp) * LANES
                yattn_ref[r0:r0 + BLOCK, lo:lo + LANES] = out[pp * BLOCK:(pp + 1) * BLOCK, :]
        cj = slice(j * MXU_DIM, (j + 1) * MXU_DIM)
        plru_ref[:, cj] = jnp.dot(ylru_ref[...], w_pl_ref[:, cj], preferred_element_type=f32)

    kall_ref[0:BLOCK, :] = kall_ref[ts:ts + BLOCK, :]
    vall_ref[0:BLOCK, :] = vall_ref[ts:ts + BLOCK, :]

    for n in range(GROUPS):
        cn = slice(n * MXU_DIM, (n + 1) * MXU_DIM)
        p_attn = jnp.dot(yattn_ref[...], w_pa_ref[:, cn], preferred_element_type=f32)
        merged_ref[:, cn] = (sgl_ref[:, cn] * plru_ref[:, cn] + sga_ref[:, cn] * p_attn).astype(bf16)
    m_out = [jnp.dot(merged_ref[...], w_out_ref[:, n * MXU_DIM:(n + 1) * MXU_DIM],
                     preferred_element_type=f32) for n in range(GROUPS)]
    mean_sq = sum(jnp.sum(mo * mo, axis=1, keepdims=True) for mo in m_out) * (1.0 / D_MODEL)
    inv_rms = lax.rsqrt(mean_sq + NORM_EPS)
    for n in range(GROUPS):
        cn = slice(n * MXU_DIM, (n + 1) * MXU_DIM)
        o_ref[:, cn] = h_ref[:, cn] + m_out[n] * inv_rms * post_g_ref[:, cn]


def _mixer(h3d, sinks, cos_t, sin_t, pre_g, w_in_p, conv_w, conv_b, w_gate, b_a, b_x, lam,
           w_pl, w_pa, w_out, post_g):
    bsz, s, _ = h3d.shape
    ts = MIX_TOKENS
    tile = pl.BlockSpec((None, ts, D_MODEL), lambda b, t: (b, t, 0))
    rope_spec = pl.BlockSpec((ts, LANES), lambda b, t: (t, 0))
    return pl.pallas_call(
        _mixer_kernel,
        name="mixer",
        grid=(bsz, s // ts),
        in_specs=[
            pl.BlockSpec(memory_space=pltpu.SMEM),
            tile, rope_spec, rope_spec,
            _const_spec((1, D_MODEL)),
            _const_spec((D_MODEL, IN_WIDTH_P + PAD_COLS)),
            _const_spec((CONV_WIDTH, LRU_WIDTH)),
            _const_spec((1, LRU_WIDTH)),
            _const_spec((LRU_WIDTH // MXU_DIM, MXU_DIM, 2 * MXU_DIM)),
            _const_spec((1, LRU_WIDTH)),
            _const_spec((1, LRU_WIDTH)),
            _const_spec((1, LRU_WIDTH)),
            _const_spec((LRU_WIDTH, D_MODEL + PAD_COLS)),
            _const_spec((ATTN_WIDTH, D_MODEL + PAD_COLS)),
            _const_spec((D_MODEL, D_MODEL + PAD_COLS)),
            _const_spec((1, D_MODEL)),
        ],
        out_specs=tile,
        out_shape=jax.ShapeDtypeStruct(h3d.shape, jnp.float32),
        scratch_shapes=[
            pltpu.VMEM((ts, D_MODEL), jnp.bfloat16),
            pltpu.VMEM((SUBLANES + ts, LRU_WIDTH), jnp.float32),
            pltpu.VMEM((1, LRU_WIDTH), jnp.float32),
            pltpu.VMEM((ts, ATTN_WIDTH), jnp.bfloat16),
            pltpu.VMEM((BLOCK + ts, 2 * KV_WIDTH), jnp.bfloat16),
            pltpu.VMEM((BLOCK + ts, 2 * KV_WIDTH), jnp.bfloat16),
            pltpu.VMEM((ts, LRU_WIDTH), jnp.bfloat16),
            pltpu.VMEM((ts, ATTN_WIDTH), jnp.bfloat16),
            pltpu.VMEM((ts, D_MODEL), jnp.float32),
            pltpu.VMEM((ts, D_MODEL), jnp.float32),
            pltpu.VMEM((ts, D_MODEL), jnp.float32),
            pltpu.VMEM((ts, D_MODEL), jnp.bfloat16),
        ],
        compiler_params=pltpu.CompilerParams(
            dimension_semantics=("arbitrary", "arbitrary"), vmem_limit_bytes=VMEM_LIMIT_BYTES),
    )(sinks, h3d, cos_t, sin_t, pre_g, w_in_p, conv_w, conv_b, w_gate, b_a, b_x, lam,
      w_pl, w_pa, w_out, post_g)


def _pad_cast(w):
    return jnp.pad(w.astype(jnp.bfloat16), ((0, 0), (0, PAD_COLS)))


def _relayout_w_in(w_in):
    d = w_in.shape[0]
    o = 2 * LRU_WIDTH
    w_gate_br = w_in[:, :LRU_WIDTH].reshape(d, GROUPS, MXU_DIM)
    w_x = w_in[:, LRU_WIDTH:o].reshape(d, GROUPS, MXU_DIM)
    wq = w_in[:, o:o + ATTN_WIDTH].reshape(d, N_Q_HEADS // 2, 2, 2, HALF)
    wq = wq.transpose(0, 1, 3, 2, 4).reshape(d, GROUPS, MXU_DIM)
    o += ATTN_WIDTH
    wk = w_in[:, o:o + KV_WIDTH].reshape(d, N_KV_HEADS, 2, 1, HALF)
    wk = jnp.broadcast_to(wk, (d, N_KV_HEADS, 2, 2, HALF)).reshape(d, N_KV_HEADS, LANES)
    o += KV_WIDTH
    wv = w_in[:, o:o + KV_WIDTH].reshape(d, N_KV_HEADS, 1, HEAD_DIM)
    wv = jnp.broadcast_to(wv, (d, N_KV_HEADS, 2, HEAD_DIM)).reshape(d, N_KV_HEADS, LANES)
    o += KV_WIDTH
    w_gl = w_in[:, o:o + D_MODEL].reshape(d, GROUPS, MXU_DIM)
    w_ga = w_in[:, o + D_MODEL:].reshape(d, GROUPS, MXU_DIM)
    blocks = [w_x, w_gate_br, wq, jnp.concatenate([wk, wv], axis=2), w_gl, w_ga]
    return _pad_cast(jnp.stack(blocks, axis=2).reshape(d, IN_WIDTH_P))


def _gate_weights(w_a, w_x):
    per = MXU_DIM // LRU_BLOCK_W
    eye = jnp.eye(per, dtype=w_a.dtype)

    def expand(w):
        w4 = w.reshape(LRU_BLOCKS // per, per, LRU_BLOCK_W, LRU_BLOCK_W)
        return jnp.einsum("gncd,nm->gncmd", w4, eye).reshape(LRU_BLOCKS // per, MXU_DIM, MXU_DIM)

    return jnp.concatenate([expand(w_a), expand(w_x)], axis=2).astype(jnp.bfloat16)


def _rope_tables(seq_len):
    inv_freq = ROPE_THETA ** (-jnp.arange(HALF, dtype=jnp.float32) / HALF)
    ang = jnp.arange(seq_len, dtype=jnp.float32)[:, None] * inv_freq[None, :]
    cos, sin = jnp.cos(ang), jnp.sin(ang)
    return (jnp.concatenate([cos, cos, cos, cos], axis=1),
            jnp.concatenate([-sin, -sin, sin, sin], axis=1))


def kernel(x, ffn1_pre_g, ffn1_w_gu, ffn1_w_down, ffn1_post_g, mix_pre_g, w_in, conv_w, conv_b, lru_w_a, lru_b_a, lru_w_x, lru_b_x, lru_lambda, attn_sinks, w_proj_lru, w_proj_attn, w_out, mix_post_g, ffn2_pre_g, ffn2_w_gu, ffn2_w_down, ffn2_post_g):
    bsz, s, d = x.shape
    depth = ffn1_pre_g.shape[0]
    cos_t, sin_t = _rope_tables(s)
    bf = jnp.bfloat16
    h = x
    for l in range(depth):
        h = _ffn(h.reshape(bsz * s, d), ffn1_pre_g[l][None], ffn1_w_gu[l].astype(bf),
                 _pad_cast(ffn1_w_down[l]), ffn1_post_g[l][None]).reshape(bsz, s, d)
        h = _mixer(h, attn_sinks[l], cos_t, sin_t, mix_pre_g[l][None], _relayout_w_in(w_in[l]),
                   conv_w[l], conv_b[l][None], _gate_weights(lru_w_a[l], lru_w_x[l]),
                   lru_b_a[l][None], lru_b_x[l][None], lru_lambda[l][None],
                   _pad_cast(w_proj_lru[l]), _pad_cast(w_proj_attn[l]), _pad_cast(w_out[l]),
                   mix_post_g[l][None])
        h = _ffn(h.reshape(bsz * s, d), ffn2_pre_g[l][None], ffn2_w_gu[l].astype(bf),
                 _pad_cast(ffn2_w_down[l]), ffn2_post_g[l][None]).reshape(bsz, s, d)
    return h
```

```python
import jax
import jax.numpy as jnp
from jax import lax
from jax.experimental import pallas as pl
from jax.experimental.pallas import tpu as pltpu

D_MODEL = 1024
LRU_WIDTH = 1024
LRU_BLOCKS = 16
LRU_BLOCK_W = LRU_WIDTH // LRU_BLOCKS
CONV_WIDTH = 4
LRU_C = 8.0
HEAD_DIM = 64
HALF = HEAD_DIM // 2
N_Q_HEADS = 16
N_KV_HEADS = 4
ATTN_WIDTH = N_Q_HEADS * HEAD_DIM
KV_WIDTH = N_KV_HEADS * HEAD_DIM
WINDOW = 128
BLOCK = 128
ROPE_THETA = 10000.0
D_FF = 2816
MACARON_SCALE = 0.5
NORM_EPS = 1e-6
MASK_VALUE = -1e30

LANES = 128
SUBLANES = 8
MXU_DIM = 256
VMEM_LIMIT_BYTES = 56 * 1024 * 1024

FFN_TOKENS = 512
FFN_CHUNK = 256
MIX_TOKENS = 256

GROUPS = LRU_WIDTH // MXU_DIM
C_GATE = 0
C_X = C_GATE + LRU_WIDTH
C_Q = C_X + LRU_WIDTH
C_K = C_Q + ATTN_WIDTH
C_V = C_K + KV_WIDTH
C_GL = C_V + KV_WIDTH
C_GA = C_GL + D_MODEL
IN_WIDTH = C_GA + D_MODEL
LOG2E = 1.4426950408889634
P_LRU, P_ATTN, P_OUT = 0, D_MODEL, 2 * D_MODEL
KV_SLAB_WIDTH = N_KV_HEADS * 2 * LANES
PAD_COLS = LANES


def _rms_norm(x, g):
    return x * lax.rsqrt(jnp.mean(x * x, axis=-1, keepdims=True) + NORM_EPS) * g


def _const_spec(shape):
    n = len(shape)
    return pl.BlockSpec(shape, lambda *_: (0,) * n, pipeline_mode=pl.Buffered(1))


def _ffn_kernel(h_ref, pre_g_ref, w_gu_ref, w_down_ref, post_g_ref, o_ref, act_ref):
    h = h_ref[...]
    xn = _rms_norm(h, pre_g_ref[...]).astype(jnp.bfloat16)
    for c in range(D_FF // FFN_CHUNK):
        lo = c * FFN_CHUNK
        g = jnp.dot(xn, w_gu_ref[:, lo:lo + FFN_CHUNK], preferred_element_type=jnp.float32)
        u = jnp.dot(xn, w_gu_ref[:, D_FF + lo:D_FF + lo + FFN_CHUNK],
                    preferred_element_type=jnp.float32)
        act_ref[:, lo:lo + FFN_CHUNK] = (jax.nn.silu(g) * u).astype(jnp.bfloat16)
    f = [jnp.dot(act_ref[...], w_down_ref[:, n * MXU_DIM:(n + 1) * MXU_DIM],
                 preferred_element_type=jnp.float32) for n in range(D_MODEL // MXU_DIM)]
    mean_sq = sum(jnp.sum(fn * fn, axis=1, keepdims=True) for fn in f) * (1.0 / D_MODEL)
    scale = MACARON_SCALE * lax.rsqrt(mean_sq + NORM_EPS)
    for n in range(D_MODEL // MXU_DIM):
        cn = slice(n * MXU_DIM, (n + 1) * MXU_DIM)
        o_ref[:, cn] = h_ref[:, cn] + f[n] * scale * post_g_ref[:, cn]


def _ffn(h2d, pre_g, w_gu, w_down, post_g):
    m = h2d.shape[0]
    return pl.pallas_call(
        _ffn_kernel,
        name="ffn",
        grid=(m // FFN_TOKENS,),
        in_specs=[
            pl.BlockSpec((FFN_TOKENS, D_MODEL), lambda i: (i, 0)),
            _const_spec((1, D_MODEL)),
            _const_spec((D_MODEL, 2 * D_FF)),
            _const_spec((D_FF, D_MODEL + PAD_COLS)),
            _const_spec((1, D_MODEL)),
        ],
        out_specs=pl.BlockSpec((FFN_TOKENS, D_MODEL), lambda i: (i, 0)),
        out_shape=jax.ShapeDtypeStruct((m, D_MODEL), jnp.float32),
        scratch_shapes=[pltpu.VMEM((FFN_TOKENS, D_FF), jnp.bfloat16)],
        compiler_params=pltpu.CompilerParams(
            dimension_semantics=("arbitrary",), vmem_limit_bytes=VMEM_LIMIT_BYTES),
    )(h2d, pre_g, w_gu, w_down, post_g)


def _mixer_kernel(sinks_ref, h_ref, cos_ref, sin_ref, pre_g_ref, w_in_ref, conv_w_ref,
                  conv_b_ref, w_gate_ref, b_a_ref, b_x_ref, lam_ref, w_proj_ref, post_g_ref, o_ref,
                  un_ref, xbuf_ref, hstate_ref, q_ref, kall_ref, vall_ref, ylru_ref, yattn_ref,
                  sgl_ref, sga_ref, plru_ref, merged_ref):
    ts = MIX_TOKENS
    t_idx = pl.program_id(1)
    f32, bf16 = jnp.float32, jnp.bfloat16

    @pl.when(t_idx == 0)
    def _():
        xbuf_ref[0:SUBLANES, :] = jnp.zeros((SUBLANES, LRU_WIDTH), f32)
        hstate_ref[...] = jnp.zeros_like(hstate_ref)
        kall_ref[0:BLOCK, :] = jnp.zeros((BLOCK, KV_SLAB_WIDTH), bf16)
        vall_ref[0:BLOCK, :] = jnp.zeros((BLOCK, KV_SLAB_WIDTH), bf16)

    un_ref[...] = _rms_norm(h_ref[...], pre_g_ref[...]).astype(bf16)

    def proj(lo):
        return jnp.dot(un_ref[...], w_in_ref[:, lo:lo + MXU_DIM], preferred_element_type=f32)

    lane = lax.broadcasted_iota(jnp.int32, (ts, LANES), 1)
    first_half = (lane % HEAD_DIM) < HALF
    first_head = lane < HEAD_DIM

    def rope(x, c, s):
        partner = jnp.where(first_half, pltpu.roll(x, LANES - HALF, axis=1),
                            pltpu.roll(x, HALF, axis=1))
        return x * c + partner * s

    cos = cos_ref[...]
    sin = sin_ref[...]
    q_scale = (HEAD_DIM ** -0.5) * LOG2E
    cos_q = cos * q_scale
    sin_q = sin * q_scale
    row_in_group = lax.broadcasted_iota(jnp.int32, (ts, MXU_DIM), 0) % SUBLANES

    k_nat = proj(C_K)
    v_nat = proj(C_V)
    for jj in range(N_KV_HEADS // 2):
        kr = rope(k_nat[:, jj * LANES:(jj + 1) * LANES], cos, sin)
        vv = v_nat[:, jj * LANES:(jj + 1) * LANES]
        kr_swapped = pltpu.roll(kr, HEAD_DIM, axis=1)
        vv_swapped = pltpu.roll(vv, HEAD_DIM, axis=1)
        for half in range(2):
            j = 2 * jj + half
            for dst_ref, same, swapped in ((kall_ref, kr, kr_swapped), (vall_ref, vv, vv_swapped)):
                low, high = (same, swapped) if half == 0 else (swapped, same)
                dst_ref[BLOCK:BLOCK + ts, (2 * j) * LANES:(2 * j + 1) * LANES] = (
                    jnp.where(first_head, low, 0.0).astype(bf16))
                dst_ref[BLOCK:BLOCK + ts, (2 * j + 1) * LANES:(2 * j + 2) * LANES] = (
                    jnp.where(first_head, 0.0, high).astype(bf16))

    for g in range(GROUPS):
        cg = slice(g * MXU_DIM, (g + 1) * MXU_DIM)

        qg = proj(C_Q + g * MXU_DIM)
        for pp in range(2):
            lo = g * MXU_DIM + pp * LANES
            q_ref[:, lo:lo + LANES] = rope(qg[:, pp * LANES:(pp + 1) * LANES], cos_q, sin_q).astype(bf16)

        xbuf_ref[SUBLANES:SUBLANES + ts, cg] = proj(C_X + g * MXU_DIM)
        xc = conv_b_ref[:, cg]
        for k in range(CONV_WIDTH):
            off = SUBLANES - (CONV_WIDTH - 1) + k
            xc = xc + xbuf_ref[off:off + ts, cg] * conv_w_ref[k:k + 1, cg]
        xbuf_ref[0:SUBLANES, cg] = xbuf_ref[ts:ts + SUBLANES, cg]
        xc_b = xc.astype(bf16)
        r = jax.nn.sigmoid(jnp.dot(xc_b, w_gate_ref[g, :, :MXU_DIM], preferred_element_type=f32)
                           + b_a_ref[:, cg])
        i_gate = jax.nn.sigmoid(jnp.dot(xc_b, w_gate_ref[g, :, MXU_DIM:], preferred_element_type=f32)
                                + b_x_ref[:, cg])
        lam = lam_ref[:, cg]
        softplus_neg_lam = jnp.maximum(-lam, 0.0) + jnp.log1p(jnp.exp(-jnp.abs(lam)))
        log_a = (-LRU_C) * r * softplus_neg_lam
        a = jnp.exp(log_a)
        mult = jnp.sqrt(jnp.maximum(-jnp.tanh(log_a) * (a * a + 1.0), 0.0))
        b = mult * (i_gate * xc)
        shift = 1
        while shift < SUBLANES:
            keep = row_in_group >= shift
            a_prev = jnp.where(keep, pltpu.roll(a, shift, axis=0), 1.0)
            b_prev = jnp.where(keep, pltpu.roll(b, shift, axis=0), 0.0)
            b = a * b_prev + b
            a = a * a_prev
            shift *= 2
        carry = hstate_ref[:, cg]
        rows = []
        for i in range(ts // SUBLANES):
            lo = i * SUBLANES
            hg = a[lo:lo + SUBLANES, :] * carry + b[lo:lo + SUBLANES, :]
            rows.append(hg)
            carry = hg[SUBLANES - 1:SUBLANES, :]
        hstate_ref[:, cg] = carry
        h_lru = jnp.concatenate(rows, axis=0)
        ylru_ref[:, cg] = (h_lru * jax.nn.gelu(proj(C_GATE + g * MXU_DIM))).astype(bf16)

        sgl_ref[:, cg] = jax.nn.sigmoid(proj(C_GL + g * MXU_DIM))
        sga_ref[:, cg] = jax.nn.sigmoid(proj(C_GA + g * MXU_DIM))

    first_head2 = lax.broadcasted_iota(jnp.int32, (2 * BLOCK, LANES), 1) < HEAD_DIM
    qi = lax.broadcasted_iota(jnp.int32, (2 * BLOCK, 2 * BLOCK), 0) % BLOCK
    si = lax.broadcasted_iota(jnp.int32, (2 * BLOCK, 2 * BLOCK), 1)
    diff = BLOCK + qi - si
    band = (diff >= 0) & (diff < WINDOW)
    first_pair_rows = lax.broadcasted_iota(jnp.int32, (2 * BLOCK, 1), 0) < BLOCK

    for j in range(N_KV_HEADS):
        for qb in range(ts // BLOCK):
            r0 = qb * BLOCK
            q2 = jnp.concatenate(
                [q_ref[r0:r0 + BLOCK, (2 * j + pp) * LANES:(2 * j + pp + 1) * LANES] for pp in range(2)],
                axis=0)
            mask = band & ((si >= BLOCK) | (t_idx * (ts // BLOCK) + qb > 0))
            parts = []
            for second in range(2):
                slab = slice((2 * j + second) * LANES, (2 * j + second + 1) * LANES)
                k_half = kall_ref[r0:r0 + 2 * BLOCK, slab]
                v_half = vall_ref[r0:r0 + 2 * BLOCK, slab]
                head = 4 * j + second
                sink = jnp.where(first_pair_rows, sinks_ref[head] * LOG2E, sinks_ref[head + 2] * LOG2E)
                s = lax.dot_general(q2, k_half, (((1,), (1,)), ((), ())), preferred_element_type=f32)
                s = jnp.where(mask, s, MASK_VALUE)
                m = jnp.maximum(jnp.max(s, axis=1, keepdims=True), sink)
                e = jnp.exp2(s - m)
                denom = jnp.sum(e, axis=1, keepdims=True) + jnp.exp2(sink - m)
                parts.append((jnp.dot(e.astype(bf16), v_half, preferred_element_type=f32), 1.0 / denom))
            (pv0, inv0), (pv1, inv1) = parts
            out = ((pv0 + pv1) * jnp.where(first_head2, inv0, inv1)).astype(bf16)
            for pp in range(2):
                lo = (2 * j + pp) * LANES
                yattn_ref[r0:r0 + BLOCK, lo:lo + LANES] = out[pp * BLOCK:(pp + 1) * BLOCK, :]
        cj = slice(j * MXU_DIM, (j + 1) * MXU_DIM)
        plru_ref[:, cj] = jnp.dot(ylru_ref[...], w_proj_ref[:, P_LRU + j * MXU_DIM:P_LRU + (j + 1) * MXU_DIM],
                                  preferred_element_type=f32)

    kall_ref[0:BLOCK, :] = kall_ref[ts:ts + BLOCK, :]
    vall_ref[0:BLOCK, :] = vall_ref[ts:ts + BLOCK, :]

    for n in range(GROUPS):
        cn = slice(n * MXU_DIM, (n + 1) * MXU_DIM)
        p_attn = jnp.dot(yattn_ref[...], w_proj_ref[:, P_ATTN + n * MXU_DIM:P_ATTN + (n + 1) * MXU_DIM],
                         preferred_element_type=f32)
        merged_ref[:, cn] = (sgl_ref[:, cn] * plru_ref[:, cn] + sga_ref[:, cn] * p_attn).astype(bf16)
    m_out = [jnp.dot(merged_ref[...], w_proj_ref[:, P_OUT + n * MXU_DIM:P_OUT + (n + 1) * MXU_DIM],
                     preferred_element_type=f32) for n in range(GROUPS)]
    mean_sq = sum(jnp.sum(mo * mo, axis=1, keepdims=True) for mo in m_out) * (1.0 / D_MODEL)
    inv_rms = lax.rsqrt(mean_sq + NORM_EPS)
    for n in range(GROUPS):
        cn = slice(n * MXU_DIM, (n + 1) * MXU_DIM)
        o_ref[:, cn] = h_ref[:, cn] + m_out[n] * inv_rms * post_g_ref[:, cn]


def _mixer(h3d, sinks, cos_t, sin_t, pre_g, w_in_p, conv_w, conv_b, w_gate, b_a, b_x, lam,
           w_proj, post_g):
    bsz, s, _ = h3d.shape
    ts = MIX_TOKENS
    tile = pl.BlockSpec((None, ts, D_MODEL), lambda b, t: (b, t, 0))
    rope_spec = pl.BlockSpec((ts, LANES), lambda b, t: (t, 0))
    return pl.pallas_call(
        _mixer_kernel,
        name="mixer",
        grid=(bsz, s // ts),
        in_specs=[
            pl.BlockSpec(memory_space=pltpu.SMEM),
            tile, rope_spec, rope_spec,
            _const_spec((1, D_MODEL)),
            _const_spec((D_MODEL, IN_WIDTH)),
            _const_spec((CONV_WIDTH, LRU_WIDTH)),
            _const_spec((1, LRU_WIDTH)),
            _const_spec((LRU_WIDTH // MXU_DIM, MXU_DIM, 2 * MXU_DIM)),
            _const_spec((1, LRU_WIDTH)),
            _const_spec((1, LRU_WIDTH)),
            _const_spec((1, LRU_WIDTH)),
            _const_spec((D_MODEL, 3 * D_MODEL + PAD_COLS)),
            _const_spec((1, D_MODEL)),
        ],
        out_specs=tile,
        out_shape=jax.ShapeDtypeStruct(h3d.shape, jnp.float32),
        scratch_shapes=[
            pltpu.VMEM((ts, D_MODEL), jnp.bfloat16),
            pltpu.VMEM((SUBLANES + ts, LRU_WIDTH), jnp.float32),
            pltpu.VMEM((1, LRU_WIDTH), jnp.float32),
            pltpu.VMEM((ts, ATTN_WIDTH), jnp.bfloat16),
            pltpu.VMEM((BLOCK + ts, KV_SLAB_WIDTH), jnp.bfloat16),
            pltpu.VMEM((BLOCK + ts, KV_SLAB_WIDTH), jnp.bfloat16),
            pltpu.VMEM((ts, LRU_WIDTH), jnp.bfloat16),
            pltpu.VMEM((ts, ATTN_WIDTH), jnp.bfloat16),
            pltpu.VMEM((ts, D_MODEL), jnp.float32),
            pltpu.VMEM((ts, D_MODEL), jnp.float32),
            pltpu.VMEM((ts, D_MODEL), jnp.float32),
            pltpu.VMEM((ts, D_MODEL), jnp.bfloat16),
        ],
        compiler_params=pltpu.CompilerParams(
            dimension_semantics=("arbitrary", "arbitrary"), vmem_limit_bytes=VMEM_LIMIT_BYTES),
    )(sinks, h3d, cos_t, sin_t, pre_g, w_in_p, conv_w, conv_b, w_gate, b_a, b_x, lam,
      w_proj, post_g)


def _pad_cast(*ws):
    zeros = jnp.zeros((ws[0].shape[0], PAD_COLS), jnp.bfloat16)
    return jnp.concatenate([w.astype(jnp.bfloat16) for w in ws] + [zeros], axis=1)


def _gate_weights(w_a, w_x):
    per = MXU_DIM // LRU_BLOCK_W
    eye = jnp.eye(per, dtype=w_a.dtype)

    def expand(w):
        w4 = w.reshape(LRU_BLOCKS // per, per, LRU_BLOCK_W, LRU_BLOCK_W)
        return jnp.einsum("gncd,nm->gncmd", w4, eye).reshape(LRU_BLOCKS // per, MXU_DIM, MXU_DIM)

    return jnp.concatenate([expand(w_a), expand(w_x)], axis=2).astype(jnp.bfloat16)


def _rope_tables(seq_len):
    inv_freq = ROPE_THETA ** (-jnp.arange(HALF, dtype=jnp.float32) / HALF)
    ang = jnp.arange(seq_len, dtype=jnp.float32)[:, None] * inv_freq[None, :]
    cos, sin = jnp.cos(ang), jnp.sin(ang)
    return (jnp.concatenate([cos, cos, cos, cos], axis=1),
            jnp.concatenate([-sin, sin, -sin, sin], axis=1))


def kernel(x, ffn1_pre_g, ffn1_w_gu, ffn1_w_down, ffn1_post_g, mix_pre_g, w_in, conv_w, conv_b, lru_w_a, lru_b_a, lru_w_x, lru_b_x, lru_lambda, attn_sinks, w_proj_lru, w_proj_attn, w_out, mix_post_g, ffn2_pre_g, ffn2_w_gu, ffn2_w_down, ffn2_post_g):
    bsz, s, d = x.shape
    depth = ffn1_pre_g.shape[0]
    cos_t, sin_t = _rope_tables(s)
    bf = jnp.bfloat16
    h = x
    for l in range(depth):
        h = _ffn(h.reshape(bsz * s, d), ffn1_pre_g[l][None], ffn1_w_gu[l].astype(bf),
                 _pad_cast(ffn1_w_down[l]), ffn1_post_g[l][None]).reshape(bsz, s, d)
        h = _mixer(h, attn_sinks[l], cos_t, sin_t, mix_pre_g[l][None], w_in[l].astype(bf),
                   conv_w[l], conv_b[l][None], _gate_weights(lru_w_a[l], lru_w_x[l]),
                   lru_b_a[l][None], lru_b_x[l][None], lru_lambda[l][None],
                   _pad_cast(w_proj_lru[l], w_proj_attn[l], w_out[l]),
                   mix_post_g[l][None])
        h = _ffn(h.reshape(bsz * s, d), ffn2_pre_g[l][None], ffn2_w_gu[l].astype(bf),
                 _pad_cast(ffn2_w_down[l]), ffn2_post_g[l][None]).reshape(bsz, s, d)
    return h
```

```python
import jax
import jax.numpy as jnp
from jax import lax
from jax.experimental import pallas as pl
from jax.experimental.pallas import tpu as pltpu

D_MODEL = 1024
LRU_WIDTH = 1024
LRU_BLOCKS = 16
LRU_BLOCK_W = LRU_WIDTH // LRU_BLOCKS
CONV_WIDTH = 4
LRU_C = 8.0
HEAD_DIM = 64
HALF = HEAD_DIM // 2
N_Q_HEADS = 16
N_KV_HEADS = 4
ATTN_WIDTH = N_Q_HEADS * HEAD_DIM
KV_WIDTH = N_KV_HEADS * HEAD_DIM
WINDOW = 128
BLOCK = 128
ROPE_THETA = 10000.0
D_FF = 2816
MACARON_SCALE = 0.5
NORM_EPS = 1e-6
MASK_VALUE = -1e30

LANES = 128
SUBLANES = 8
MXU_DIM = 256
VMEM_LIMIT_BYTES = 56 * 1024 * 1024

FFN_TOKENS = 512
FFN_CHUNK = 256
MIX_TOKENS = 256
SEG_LEN = SUBLANES

GROUP_W = 2 * MXU_DIM
GROUPS = LRU_WIDTH // GROUP_W
C_GATE = 0
C_X = C_GATE + LRU_WIDTH
C_Q = C_X + LRU_WIDTH
C_K = C_Q + ATTN_WIDTH
C_V = C_K + KV_WIDTH
C_GL = C_V + KV_WIDTH
C_GA = C_GL + D_MODEL
IN_WIDTH = C_GA + D_MODEL
LOG2E = 1.4426950408889634
P_LRU, P_ATTN, P_OUT = 0, D_MODEL, 2 * D_MODEL
KV_SLAB_WIDTH = N_KV_HEADS * 2 * LANES
PAD_COLS = LANES


def _rms_norm(x, g):
    return x * lax.rsqrt(jnp.mean(x * x, axis=-1, keepdims=True) + NORM_EPS) * g


def _const_spec(shape):
    n = len(shape)
    return pl.BlockSpec(shape, lambda *_: (0,) * n, pipeline_mode=pl.Buffered(1))


def _ffn_kernel(h_ref, pre_g_ref, w_gu_ref, w_down_ref, post_g_ref, o_ref, act_ref):
    h = h_ref[...]
    xn = _rms_norm(h, pre_g_ref[...]).astype(jnp.bfloat16)
    for c in range(D_FF // FFN_CHUNK):
        lo = c * FFN_CHUNK
        g = jnp.dot(xn, w_gu_ref[:, lo:lo + FFN_CHUNK], preferred_element_type=jnp.float32)
        u = jnp.dot(xn, w_gu_ref[:, D_FF + lo:D_FF + lo + FFN_CHUNK],
                    preferred_element_type=jnp.float32)
        act_ref[:, lo:lo + FFN_CHUNK] = (jax.nn.silu(g) * u).astype(jnp.bfloat16)
    f = [jnp.dot(act_ref[...], w_down_ref[:, n * GROUP_W:(n + 1) * GROUP_W],
                 preferred_element_type=jnp.float32) for n in range(D_MODEL // GROUP_W)]
    mean_sq = sum(jnp.sum(fn * fn, axis=1, keepdims=True) for fn in f) * (1.0 / D_MODEL)
    scale = MACARON_SCALE * lax.rsqrt(mean_sq + NORM_EPS)
    for n in range(D_MODEL // GROUP_W):
        cn = slice(n * GROUP_W, (n + 1) * GROUP_W)
        o_ref[:, cn] = h_ref[:, cn] + f[n] * scale * post_g_ref[:, cn]


def _ffn(h2d, pre_g, w_gu, w_down, post_g):
    m = h2d.shape[0]
    return pl.pallas_call(
        _ffn_kernel,
        name="ffn",
        grid=(m // FFN_TOKENS,),
        in_specs=[
            pl.BlockSpec((FFN_TOKENS, D_MODEL), lambda i: (i, 0)),
            _const_spec((1, D_MODEL)),
            _const_spec((D_MODEL, 2 * D_FF)),
            _const_spec((D_FF, D_MODEL + PAD_COLS)),
            _const_spec((1, D_MODEL)),
        ],
        out_specs=pl.BlockSpec((FFN_TOKENS, D_MODEL), lambda i: (i, 0)),
        out_shape=jax.ShapeDtypeStruct((m, D_MODEL), jnp.float32),
        scratch_shapes=[pltpu.VMEM((FFN_TOKENS, D_FF), jnp.bfloat16)],
        compiler_params=pltpu.CompilerParams(
            dimension_semantics=("arbitrary",), vmem_limit_bytes=VMEM_LIMIT_BYTES),
    )(h2d, pre_g, w_gu, w_down, post_g)


def _mixer_kernel(sinks_ref, h_ref, cos_ref, sin_ref, pre_g_ref, w_in_ref, conv_w_ref,
                  conv_b_ref, w_gate_ref, b_a_ref, b_x_ref, lam_ref, w_proj_ref, post_g_ref, o_ref,
                  un_ref, unf_ref, unp_ref, xhist_ref, hstate_ref, q_ref, kall_ref, vall_ref, ylru_ref,
                  yattn_ref, sgl_ref, sga_ref, plru_ref, merged_ref):
    ts = MIX_TOKENS
    seg_rows = ts // SEG_LEN
    t_idx = pl.program_id(1)
    f32, bf16 = jnp.float32, jnp.bfloat16

    @pl.when(t_idx == 0)
    def _():
        xhist_ref[...] = jnp.zeros_like(xhist_ref)
        hstate_ref[...] = jnp.zeros_like(hstate_ref)
        kall_ref[0:BLOCK, :] = jnp.zeros((BLOCK, KV_SLAB_WIDTH), bf16)
        vall_ref[0:BLOCK, :] = jnp.zeros((BLOCK, KV_SLAB_WIDTH), bf16)

    un = _rms_norm(h_ref[...], pre_g_ref[...])
    un_ref[...] = un.astype(bf16)
    for lg in range(D_MODEL // LANES):
        unf_ref[lg] = un[:, lg * LANES:(lg + 1) * LANES]
    for lg in range(D_MODEL // LANES):
        pieces = [unf_ref.at[lg][pl.ds(SEG_LEN * SUBLANES * grp + i, SUBLANES, stride=SEG_LEN), :]
                  for i in range(SEG_LEN) for grp in range(seg_rows // SUBLANES)]
        unp_ref[:, lg * LANES:(lg + 1) * LANES] = jnp.concatenate(pieces, axis=0).astype(bf16)

    def proj(lo, src_ref=un_ref):
        return jnp.dot(src_ref[...], w_in_ref[:, lo:lo + GROUP_W], preferred_element_type=f32)

    lane = lax.broadcasted_iota(jnp.int32, (ts, LANES), 1)
    first_half = (lane % HEAD_DIM) < HALF
    first_head = lane < HEAD_DIM

    def rope(x, c, s):
        partner = jnp.where(first_half, pltpu.roll(x, LANES - HALF, axis=1),
                            pltpu.roll(x, HALF, axis=1))
        return x * c + partner * s

    cos = cos_ref[...]
    sin = sin_ref[...]
    q_scale = (HEAD_DIM ** -0.5) * LOG2E
    cos_q = cos * q_scale
    sin_q = sin * q_scale
    seg_row = lax.broadcasted_iota(jnp.int32, (seg_rows, GROUP_W), 0)

    def shift_rows(cur, first_row):
        return jnp.where(seg_row == 0, first_row, pltpu.roll(cur, 1, axis=0))

    kv_nat = proj(C_K)
    k_nat, v_nat = kv_nat[:, :KV_WIDTH], kv_nat[:, KV_WIDTH:]
    for jj in range(N_KV_HEADS // 2):
        kr = rope(k_nat[:, jj * LANES:(jj + 1) * LANES], cos, sin)
        vv = v_nat[:, jj * LANES:(jj + 1) * LANES]
        kr_swapped = pltpu.roll(kr, HEAD_DIM, axis=1)
        vv_swapped = pltpu.roll(vv, HEAD_DIM, axis=1)
        for half in range(2):
            j = 2 * jj + half
            for dst_ref, same, swapped in ((kall_ref, kr, kr_swapped), (vall_ref, vv, vv_swapped)):
                low, high = (same, swapped) if half == 0 else (swapped, same)
                dst_ref[BLOCK:BLOCK + ts, (2 * j) * LANES:(2 * j + 1) * LANES] = (
                    jnp.where(first_head, low, 0.0).astype(bf16))
                dst_ref[BLOCK:BLOCK + ts, (2 * j + 1) * LANES:(2 * j + 2) * LANES] = (
                    jnp.where(first_head, 0.0, high).astype(bf16))

    for g in range(GROUPS):
        cg = slice(g * GROUP_W, (g + 1) * GROUP_W)

        qg = proj(C_Q + g * GROUP_W)
        for pp in range(GROUP_W // LANES):
            lo = g * GROUP_W + pp * LANES
            q_ref[:, lo:lo + LANES] = rope(qg[:, pp * LANES:(pp + 1) * LANES], cos_q, sin_q).astype(bf16)

        xg = proj(C_X + g * GROUP_W, unp_ref)
        x_blk = [xg[i * seg_rows:(i + 1) * seg_rows, :] for i in range(SEG_LEN)]
        wrapped = {}
        for i_src in range(SEG_LEN - CONV_WIDTH + 1, SEG_LEN):
            hist = xhist_ref[i_src - (SEG_LEN - CONV_WIDTH + 1), :, cg]
            wrapped[i_src] = shift_rows(x_blk[i_src], hist)
            xhist_ref[i_src - (SEG_LEN - CONV_WIDTH + 1), :, cg] = x_blk[i_src][seg_rows - 1:seg_rows, :]
        xc_blk = []
        for i in range(SEG_LEN):
            acc = conv_b_ref[:, cg]
            for d in range(CONV_WIDTH):
                src = x_blk[i - d] if i >= d else wrapped[i - d + SEG_LEN]
                acc = acc + src * conv_w_ref[CONV_WIDTH - 1 - d:CONV_WIDTH - d, cg]
            xc_blk.append(acc)
        xc = jnp.concatenate(xc_blk, axis=0)
        xc_b = xc.astype(bf16)
        gates = [jnp.dot(xc_b[:, k * MXU_DIM:(k + 1) * MXU_DIM], w_gate_ref[g * (GROUP_W // MXU_DIM) + k],
                         preferred_element_type=f32) for k in range(GROUP_W // MXU_DIM)]
        r = jax.nn.sigmoid(jnp.concatenate([ri[:, :MXU_DIM] for ri in gates], axis=1) + b_a_ref[:, cg])
        i_gate = jax.nn.sigmoid(jnp.concatenate([ri[:, MXU_DIM:] for ri in gates], axis=1)
                                + b_x_ref[:, cg])
        lam = lam_ref[:, cg]
        softplus_neg_lam = jnp.maximum(-lam, 0.0) + jnp.log1p(jnp.exp(-jnp.abs(lam)))
        log_a = (-LRU_C) * r * softplus_neg_lam
        a = jnp.exp(log_a)
        t = -jnp.tanh(log_a) * (a * a + 1.0)
        mult = jnp.where(t > 0.0, t * lax.rsqrt(t), 0.0)
        b = mult * (i_gate * xc)
        a_blk = [a[i * seg_rows:(i + 1) * seg_rows, :] for i in range(SEG_LEN)]
        b_blk = [b[i * seg_rows:(i + 1) * seg_rows, :] for i in range(SEG_LEN)]
        h_loc, a_cum = [b_blk[0]], [a_blk[0]]
        for i in range(1, SEG_LEN):
            h_loc.append(a_blk[i] * h_loc[-1] + b_blk[i])
            a_cum.append(a_blk[i] * a_cum[-1])
        seg_a, seg_h = a_cum[-1], h_loc[-1]
        shift = 1
        while shift < SUBLANES:
            keep = (seg_row % SUBLANES) >= shift
            a_prev = jnp.where(keep, pltpu.roll(seg_a, shift, axis=0), 1.0)
            h_prev = jnp.where(keep, pltpu.roll(seg_h, shift, axis=0), 0.0)
            seg_h = seg_a * h_prev + seg_h
            seg_a = seg_a * a_prev
            shift *= 2
        carry = hstate_ref[:, cg]
        state_in = carry
        ends = []
        for v in range(seg_rows // SUBLANES):
            lo = v * SUBLANES
            hv = seg_a[lo:lo + SUBLANES, :] * carry + seg_h[lo:lo + SUBLANES, :]
            ends.append(hv)
            carry = hv[SUBLANES - 1:SUBLANES, :]
        hstate_ref[:, cg] = carry
        seg_in = shift_rows(jnp.concatenate(ends, axis=0), state_in)
        h_lru = jnp.concatenate([h_loc[i] + a_cum[i] * seg_in for i in range(SEG_LEN)], axis=0)
        ylru_ref[:, cg] = (h_lru * jax.nn.gelu(proj(C_GATE + g * GROUP_W, unp_ref))).astype(bf16)

        sgl_ref[:, cg] = jax.nn.sigmoid(proj(C_GL + g * GROUP_W))
        sga_ref[:, cg] = jax.nn.sigmoid(proj(C_GA + g * GROUP_W))

    first_head2 = lax.broadcasted_iota(jnp.int32, (2 * BLOCK, LANES), 1) < HEAD_DIM
    qi = lax.broadcasted_iota(jnp.int32, (2 * BLOCK, 2 * BLOCK), 0) % BLOCK
    si = lax.broadcasted_iota(jnp.int32, (2 * BLOCK, 2 * BLOCK), 1)
    diff = BLOCK + qi - si
    band = (diff >= 0) & (diff < WINDOW)
    first_pair_rows = lax.broadcasted_iota(jnp.int32, (2 * BLOCK, 1), 0) < BLOCK

    for j in range(N_KV_HEADS):
        for qb in range(ts // BLOCK):
            r0 = qb * BLOCK
            q2 = jnp.concatenate(
                [q_ref[r0:r0 + BLOCK, (2 * j + pp) * LANES:(2 * j + pp + 1) * LANES] for pp in range(2)],
                axis=0)
            mask = band & ((si >= BLOCK) | (t_idx * (ts // BLOCK) + qb > 0))
            parts = []
            for second in range(2):
                slab = slice((2 * j + second) * LANES, (2 * j + second + 1) * LANES)
                k_half = kall_ref[r0:r0 + 2 * BLOCK, slab]
                v_half = vall_ref[r0:r0 + 2 * BLOCK, slab]
                head = 4 * j + second
                sink = jnp.where(first_pair_rows, sinks_ref[head] * LOG2E, sinks_ref[head + 2] * LOG2E)
                s = lax.dot_general(q2, k_half, (((1,), (1,)), ((), ())), preferred_element_type=f32)
                s = jnp.where(mask, s, MASK_VALUE)
                m = jnp.maximum(jnp.max(s, axis=1, keepdims=True), sink)
                e = jnp.exp2(s - m)
                denom = jnp.sum(e, axis=1, keepdims=True) + jnp.exp2(sink - m)
                parts.append((jnp.dot(e.astype(bf16), v_half, preferred_element_type=f32), 1.0 / denom))
            (pv0, inv0), (pv1, inv1) = parts
            out = ((pv0 + pv1) * jnp.where(first_head2, inv0, inv1)).astype(bf16)
            for pp in range(2):
                lo = (2 * j + pp) * LANES
                yattn_ref[r0:r0 + BLOCK, lo:lo + LANES] = out[pp * BLOCK:(pp + 1) * BLOCK, :]
        if j % 2 == 0:
            continue
        c = j // 2
        p_lru = jnp.dot(ylru_ref[...], w_proj_ref[:, P_LRU + c * GROUP_W:P_LRU + (c + 1) * GROUP_W],
                        preferred_element_type=f32)
        for half in range(GROUP_W // LANES):
            lg = c * (GROUP_W // LANES) + half
            for i in range(SEG_LEN):
                for grp in range(seg_rows // SUBLANES):
                    row = i * seg_rows + grp * SUBLANES
                    plru_ref.at[lg][pl.ds(SEG_LEN * SUBLANES * grp + i, SUBLANES, stride=SEG_LEN), :] = (
                        p_lru[row:row + SUBLANES, half * LANES:(half + 1) * LANES])

    kall_ref[0:BLOCK, :] = kall_ref[ts:ts + BLOCK, :]
    vall_ref[0:BLOCK, :] = vall_ref[ts:ts + BLOCK, :]

    for n in range(GROUPS):
        cn = slice(n * GROUP_W, (n + 1) * GROUP_W)
        p_attn = jnp.dot(yattn_ref[...], w_proj_ref[:, P_ATTN + n * GROUP_W:P_ATTN + (n + 1) * GROUP_W],
                         preferred_element_type=f32)
        p_lru = jnp.concatenate([plru_ref[n * (GROUP_W // LANES) + half] for half in range(GROUP_W // LANES)],
                                axis=1)
        merged_ref[:, cn] = (sgl_ref[:, cn] * p_lru + sga_ref[:, cn] * p_attn).astype(bf16)
    m_out = [jnp.dot(merged_ref[...], w_proj_ref[:, P_OUT + n * GROUP_W:P_OUT + (n + 1) * GROUP_W],
                     preferred_element_type=f32) for n in range(GROUPS)]
    mean_sq = sum(jnp.sum(mo * mo, axis=1, keepdims=True) for mo in m_out) * (1.0 / D_MODEL)
    inv_rms = lax.rsqrt(mean_sq + NORM_EPS)
    for n in range(GROUPS):
        cn = slice(n * GROUP_W, (n + 1) * GROUP_W)
        o_ref[:, cn] = h_ref[:, cn] + m_out[n] * inv_rms * post_g_ref[:, cn]


def _mixer(h3d, sinks, cos_t, sin_t, pre_g, w_in_p, conv_w, conv_b, w_gate, b_a, b_x, lam,
           w_proj, post_g):
    bsz, s, _ = h3d.shape
    ts = MIX_TOKENS
    tile = pl.BlockSpec((None, ts, D_MODEL), lambda b, t: (b, t, 0))
    rope_spec = pl.BlockSpec((ts, LANES), lambda b, t: (t, 0))
    return pl.pallas_call(
        _mixer_kernel,
        name="mixer",
        grid=(bsz, s // ts),
        in_specs=[
            pl.BlockSpec(memory_space=pltpu.SMEM),
            tile, rope_spec, rope_spec,
            _const_spec((1, D_MODEL)),
            _const_spec((D_MODEL, IN_WIDTH)),
            _const_spec((CONV_WIDTH, LRU_WIDTH)),
            _const_spec((1, LRU_WIDTH)),
            _const_spec((LRU_WIDTH // MXU_DIM, MXU_DIM, 2 * MXU_DIM)),
            _const_spec((1, LRU_WIDTH)),
            _const_spec((1, LRU_WIDTH)),
            _const_spec((1, LRU_WIDTH)),
            _const_spec((D_MODEL, 3 * D_MODEL + PAD_COLS)),
            _const_spec((1, D_MODEL)),
        ],
        out_specs=tile,
        out_shape=jax.ShapeDtypeStruct(h3d.shape, jnp.float32),
        scratch_shapes=[
            pltpu.VMEM((ts, D_MODEL), jnp.bfloat16),
            pltpu.VMEM((D_MODEL // LANES, ts, LANES), jnp.float32),
            pltpu.VMEM((ts, D_MODEL), jnp.bfloat16),
            pltpu.VMEM((CONV_WIDTH - 1, 1, LRU_WIDTH), jnp.float32),
            pltpu.VMEM((1, LRU_WIDTH), jnp.float32),
            pltpu.VMEM((ts, ATTN_WIDTH), jnp.bfloat16),
            pltpu.VMEM((BLOCK + ts, KV_SLAB_WIDTH), jnp.bfloat16),
            pltpu.VMEM((BLOCK + ts, KV_SLAB_WIDTH), jnp.bfloat16),
            pltpu.VMEM((ts, LRU_WIDTH), jnp.bfloat16),
            pltpu.VMEM((ts, ATTN_WIDTH), jnp.bfloat16),
            pltpu.VMEM((ts, D_MODEL), jnp.float32),
            pltpu.VMEM((ts, D_MODEL), jnp.float32),
            pltpu.VMEM((D_MODEL // LANES, ts, LANES), jnp.float32),
            pltpu.VMEM((ts, D_MODEL), jnp.bfloat16),
        ],
        compiler_params=pltpu.CompilerParams(
            dimension_semantics=("arbitrary", "arbitrary"), vmem_limit_bytes=VMEM_LIMIT_BYTES),
    )(sinks, h3d, cos_t, sin_t, pre_g, w_in_p, conv_w, conv_b, w_gate, b_a, b_x, lam,
      w_proj, post_g)


def _pad_cast(*ws):
    zeros = jnp.zeros((ws[0].shape[0], PAD_COLS), jnp.bfloat16)
    return jnp.concatenate([w.astype(jnp.bfloat16) for w in ws] + [zeros], axis=1)


def _gate_weights(w_a, w_x):
    per = MXU_DIM // LRU_BLOCK_W
    eye = jnp.eye(per, dtype=w_a.dtype)

    def expand(w):
        w4 = w.reshape(LRU_BLOCKS // per, per, LRU_BLOCK_W, LRU_BLOCK_W)
        return jnp.einsum("gncd,nm->gncmd", w4, eye).reshape(LRU_BLOCKS // per, MXU_DIM, MXU_DIM)

    return jnp.concatenate([expand(w_a), expand(w_x)], axis=2).astype(jnp.bfloat16)


def _rope_tables(seq_len):
    inv_freq = ROPE_THETA ** (-jnp.arange(HALF, dtype=jnp.float32) / HALF)
    ang = jnp.arange(seq_len, dtype=jnp.float32)[:, None] * inv_freq[None, :]
    cos, sin = jnp.cos(ang), jnp.sin(ang)
    return (jnp.concatenate([cos, cos, cos, cos], axis=1),
            jnp.concatenate([-sin, sin, -sin, sin], axis=1))


def kernel(x, ffn1_pre_g, ffn1_w_gu, ffn1_w_down, ffn1_post_g, mix_pre_g, w_in, conv_w, conv_b, lru_w_a, lru_b_a, lru_w_x, lru_b_x, lru_lambda, attn_sinks, w_proj_lru, w_proj_attn, w_out, mix_post_g, ffn2_pre_g, ffn2_w_gu, ffn2_w_down, ffn2_post_g):
    bsz, s, d = x.shape
    depth = ffn1_pre_g.shape[0]
    cos_t, sin_t = _rope_tables(s)
    bf = jnp.bfloat16
    h = x
    for l in range(depth):
        h = _ffn(h.reshape(bsz * s, d), ffn1_pre_g[l][None], ffn1_w_gu[l].astype(bf),
                 _pad_cast(ffn1_w_down[l]), ffn1_post_g[l][None]).reshape(bsz, s, d)
        h = _mixer(h, attn_sinks[l], cos_t, sin_t, mix_pre_g[l][None], w_in[l].astype(bf),
                   conv_w[l], conv_b[l][None], _gate_weights(lru_w_a[l], lru_w_x[l]),
                   lru_b_a[l][None], lru_b_x[l][None], lru_lambda[l][None],
                   _pad_cast(w_proj_lru[l], w_proj_attn[l], w_out[l]),
                   mix_post_g[l][None])
        h = _ffn(h.reshape(bsz * s, d), ffn2_pre_g[l][None], ffn2_w_gu[l].astype(bf),
                 _pad_cast(ffn2_w_down[l]), ffn2_post_g[l][None]).reshape(bsz, s, d)
    return h
```

```python
import jax
import jax.numpy as jnp
from jax import lax
from jax.experimental import pallas as pl
from jax.experimental.pallas import tpu as pltpu

D_MODEL = 1024
LRU_WIDTH = 1024
LRU_BLOCKS = 16
LRU_BLOCK_W = LRU_WIDTH // LRU_BLOCKS
CONV_WIDTH = 4
LRU_C = 8.0
HEAD_DIM = 64
HALF = HEAD_DIM // 2
N_Q_HEADS = 16
N_KV_HEADS = 4
ATTN_WIDTH = N_Q_HEADS * HEAD_DIM
KV_WIDTH = N_KV_HEADS * HEAD_DIM
WINDOW = 128
BLOCK = 128
ROPE_THETA = 10000.0
D_FF = 2816
MACARON_SCALE = 0.5
NORM_EPS = 1e-6
MASK_VALUE = -1e30

LANES = 128
SUBLANES = 8
MXU_DIM = 256
VMEM_LIMIT_BYTES = 56 * 1024 * 1024

FFN_TOKENS = 512
FFN_CHUNK = 256
MIX_TOKENS = 512
SEG_LEN = SUBLANES

GROUP_W = 2 * MXU_DIM
GROUPS = LRU_WIDTH // GROUP_W
C_GATE = 0
C_X = C_GATE + LRU_WIDTH
C_Q = C_X + LRU_WIDTH
C_K = C_Q + ATTN_WIDTH
C_V = C_K + KV_WIDTH
C_GL = C_V + KV_WIDTH
C_GA = C_GL + D_MODEL
IN_WIDTH = C_GA + D_MODEL
LOG2E = 1.4426950408889634
P_LRU, P_ATTN, P_OUT = 0, D_MODEL, 2 * D_MODEL
KV_SLAB_WIDTH = N_KV_HEADS * 2 * LANES
PAD_COLS = LANES


def _rms_norm(x, g):
    return x * lax.rsqrt(jnp.mean(x * x, axis=-1, keepdims=True) + NORM_EPS) * g


def _const_spec(shape):
    n = len(shape)
    return pl.BlockSpec(shape, lambda *_: (0,) * n, pipeline_mode=pl.Buffered(1))


def _ffn_kernel(h_ref, pre_g_ref, w_gu_ref, w_down_ref, post_g_ref, o_ref, act_ref):
    h = h_ref[...]
    xn = _rms_norm(h, pre_g_ref[...]).astype(jnp.bfloat16)
    for c in range(D_FF // FFN_CHUNK):
        lo = c * FFN_CHUNK
        g = jnp.dot(xn, w_gu_ref[:, lo:lo + FFN_CHUNK], preferred_element_type=jnp.float32)
        u = jnp.dot(xn, w_gu_ref[:, D_FF + lo:D_FF + lo + FFN_CHUNK],
                    preferred_element_type=jnp.float32)
        act_ref[:, lo:lo + FFN_CHUNK] = (jax.nn.silu(g) * u).astype(jnp.bfloat16)
    f = [jnp.dot(act_ref[...], w_down_ref[:, n * GROUP_W:(n + 1) * GROUP_W],
                 preferred_element_type=jnp.float32) for n in range(D_MODEL // GROUP_W)]
    mean_sq = sum(jnp.sum(fn * fn, axis=1, keepdims=True) for fn in f) * (1.0 / D_MODEL)
    scale = MACARON_SCALE * lax.rsqrt(mean_sq + NORM_EPS)
    for n in range(D_MODEL // GROUP_W):
        cn = slice(n * GROUP_W, (n + 1) * GROUP_W)
        o_ref[:, cn] = h_ref[:, cn] + f[n] * scale * post_g_ref[:, cn]


def _ffn(h2d, pre_g, w_gu, w_down, post_g):
    m = h2d.shape[0]
    return pl.pallas_call(
        _ffn_kernel,
        name="ffn",
        grid=(m // FFN_TOKENS,),
        in_specs=[
            pl.BlockSpec((FFN_TOKENS, D_MODEL), lambda i: (i, 0)),
            _const_spec((1, D_MODEL)),
            _const_spec((D_MODEL, 2 * D_FF)),
            _const_spec((D_FF, D_MODEL + PAD_COLS)),
            _const_spec((1, D_MODEL)),
        ],
        out_specs=pl.BlockSpec((FFN_TOKENS, D_MODEL), lambda i: (i, 0)),
        out_shape=jax.ShapeDtypeStruct((m, D_MODEL), jnp.float32),
        scratch_shapes=[pltpu.VMEM((FFN_TOKENS, D_FF), jnp.bfloat16)],
        compiler_params=pltpu.CompilerParams(
            dimension_semantics=("arbitrary",), vmem_limit_bytes=VMEM_LIMIT_BYTES),
    )(h2d, pre_g, w_gu, w_down, post_g)


def _mixer_kernel(sinks_ref, h_ref, cos_ref, sin_ref, pre_g_ref, w_in_ref, conv_w_ref,
                  conv_b_ref, w_gate_ref, b_a_ref, b_x_ref, lam_ref, w_proj_ref, post_g_ref, o_ref,
                  un_ref, unf_ref, unp_ref, xhist_ref, hstate_ref, q_ref, kall_ref, vall_ref, ylru_ref,
                  yattn_ref, sgl_ref, sga_ref, plru_ref, merged_ref):
    ts = MIX_TOKENS
    seg_rows = ts // SEG_LEN
    t_idx = pl.program_id(1)
    f32, bf16 = jnp.float32, jnp.bfloat16

    @pl.when(t_idx == 0)
    def _():
        xhist_ref[...] = jnp.zeros_like(xhist_ref)
        hstate_ref[...] = jnp.zeros_like(hstate_ref)
        kall_ref[0:BLOCK, :] = jnp.zeros((BLOCK, KV_SLAB_WIDTH), bf16)
        vall_ref[0:BLOCK, :] = jnp.zeros((BLOCK, KV_SLAB_WIDTH), bf16)

    un = _rms_norm(h_ref[...], pre_g_ref[...])
    un_ref[...] = un.astype(bf16)
    for lg in range(D_MODEL // LANES):
        unf_ref[lg] = un[:, lg * LANES:(lg + 1) * LANES]
    for lg in range(D_MODEL // LANES):
        pieces = [unf_ref.at[lg][pl.ds(SEG_LEN * SUBLANES * grp + i, SUBLANES, stride=SEG_LEN), :]
                  for i in range(SEG_LEN) for grp in range(seg_rows // SUBLANES)]
        unp_ref[:, lg * LANES:(lg + 1) * LANES] = jnp.concatenate(pieces, axis=0).astype(bf16)

    def proj(lo, src_ref=un_ref):
        return jnp.dot(src_ref[...], w_in_ref[:, lo:lo + GROUP_W], preferred_element_type=f32)

    lane = lax.broadcasted_iota(jnp.int32, (ts, LANES), 1)
    first_half = (lane % HEAD_DIM) < HALF
    first_head = lane < HEAD_DIM

    def rope(x, c, s):
        partner = jnp.where(first_half, pltpu.roll(x, LANES - HALF, axis=1),
                            pltpu.roll(x, HALF, axis=1))
        return x * c + partner * s

    cos = cos_ref[...]
    sin = sin_ref[...]
    q_scale = (HEAD_DIM ** -0.5) * LOG2E
    cos_q = cos * q_scale
    sin_q = sin * q_scale
    seg_row = lax.broadcasted_iota(jnp.int32, (seg_rows, GROUP_W), 0)

    def shift_rows(cur, first_row):
        return jnp.where(seg_row == 0, first_row, pltpu.roll(cur, 1, axis=0))

    kv_nat = proj(C_K)
    k_nat, v_nat = kv_nat[:, :KV_WIDTH], kv_nat[:, KV_WIDTH:]
    for jj in range(N_KV_HEADS // 2):
        kr = rope(k_nat[:, jj * LANES:(jj + 1) * LANES], cos, sin)
        vv = v_nat[:, jj * LANES:(jj + 1) * LANES]
        kr_swapped = pltpu.roll(kr, HEAD_DIM, axis=1)
        vv_swapped = pltpu.roll(vv, HEAD_DIM, axis=1)
        for half in range(2):
            j = 2 * jj + half
            for dst_ref, same, swapped in ((kall_ref, kr, kr_swapped), (vall_ref, vv, vv_swapped)):
                low, high = (same, swapped) if half == 0 else (swapped, same)
                dst_ref[BLOCK:BLOCK + ts, (2 * j) * LANES:(2 * j + 1) * LANES] = (
                    jnp.where(first_head, low, 0.0).astype(bf16))
                dst_ref[BLOCK:BLOCK + ts, (2 * j + 1) * LANES:(2 * j + 2) * LANES] = (
                    jnp.where(first_head, 0.0, high).astype(bf16))

    for g in range(GROUPS):
        cg = slice(g * GROUP_W, (g + 1) * GROUP_W)

        qg = proj(C_Q + g * GROUP_W)
        for pp in range(GROUP_W // LANES):
            lo = g * GROUP_W + pp * LANES
            q_ref[:, lo:lo + LANES] = rope(qg[:, pp * LANES:(pp + 1) * LANES], cos_q, sin_q).astype(bf16)

        xg = proj(C_X + g * GROUP_W, unp_ref)
        x_blk = [xg[i * seg_rows:(i + 1) * seg_rows, :] for i in range(SEG_LEN)]
        wrapped = {}
        for i_src in range(SEG_LEN - CONV_WIDTH + 1, SEG_LEN):
            hist = xhist_ref[i_src - (SEG_LEN - CONV_WIDTH + 1), :, cg]
            wrapped[i_src] = shift_rows(x_blk[i_src], hist)
            xhist_ref[i_src - (SEG_LEN - CONV_WIDTH + 1), :, cg] = x_blk[i_src][seg_rows - 1:seg_rows, :]
        xc_blk = []
        for i in range(SEG_LEN):
            acc = conv_b_ref[:, cg]
            for d in range(CONV_WIDTH):
                src = x_blk[i - d] if i >= d else wrapped[i - d + SEG_LEN]
                acc = acc + src * conv_w_ref[CONV_WIDTH - 1 - d:CONV_WIDTH - d, cg]
            xc_blk.append(acc)
        xc = jnp.concatenate(xc_blk, axis=0)
        xc_b = xc.astype(bf16)
        gates = [jnp.dot(xc_b[:, k * MXU_DIM:(k + 1) * MXU_DIM], w_gate_ref[g * (GROUP_W // MXU_DIM) + k],
                         preferred_element_type=f32) for k in range(GROUP_W // MXU_DIM)]
        r = jax.nn.sigmoid(jnp.concatenate([ri[:, :MXU_DIM] for ri in gates], axis=1) + b_a_ref[:, cg])
        i_gate = jax.nn.sigmoid(jnp.concatenate([ri[:, MXU_DIM:] for ri in gates], axis=1)
                                + b_x_ref[:, cg])
        lam = lam_ref[:, cg]
        softplus_neg_lam = jnp.maximum(-lam, 0.0) + jnp.log1p(jnp.exp(-jnp.abs(lam)))
        log_a = (-LRU_C) * r * softplus_neg_lam
        a = jnp.exp(log_a)
        t = -jnp.tanh(log_a) * (a * a + 1.0)
        mult = jnp.where(t > 0.0, t * lax.rsqrt(t), 0.0)
        b = mult * (i_gate * xc)
        a_blk = [a[i * seg_rows:(i + 1) * seg_rows, :] for i in range(SEG_LEN)]
        b_blk = [b[i * seg_rows:(i + 1) * seg_rows, :] for i in range(SEG_LEN)]
        h_loc, a_cum = [b_blk[0]], [a_blk[0]]
        for i in range(1, SEG_LEN):
            h_loc.append(a_blk[i] * h_loc[-1] + b_blk[i])
            a_cum.append(a_blk[i] * a_cum[-1])
        seg_a, seg_h = a_cum[-1], h_loc[-1]
        shift = 1
        while shift < SUBLANES:
            keep = (seg_row % SUBLANES) >= shift
            a_prev = jnp.where(keep, pltpu.roll(seg_a, shift, axis=0), 1.0)
            h_prev = jnp.where(keep, pltpu.roll(seg_h, shift, axis=0), 0.0)
            seg_h = seg_a * h_prev + seg_h
            seg_a = seg_a * a_prev
            shift *= 2
        carry = hstate_ref[:, cg]
        state_in = carry
        ends = []
        for v in range(seg_rows // SUBLANES):
            lo = v * SUBLANES
            hv = seg_a[lo:lo + SUBLANES, :] * carry + seg_h[lo:lo + SUBLANES, :]
            ends.append(hv)
            carry = hv[SUBLANES - 1:SUBLANES, :]
        hstate_ref[:, cg] = carry
        seg_in = shift_rows(jnp.concatenate(ends, axis=0), state_in)
        h_lru = jnp.concatenate([h_loc[i] + a_cum[i] * seg_in for i in range(SEG_LEN)], axis=0)
        ylru_ref[:, cg] = (h_lru * jax.nn.gelu(proj(C_GATE + g * GROUP_W, unp_ref))).astype(bf16)

        sgl_ref[:, cg] = jax.nn.sigmoid(proj(C_GL + g * GROUP_W))
        sga_ref[:, cg] = jax.nn.sigmoid(proj(C_GA + g * GROUP_W))

    first_head2 = lax.broadcasted_iota(jnp.int32, (2 * BLOCK, LANES), 1) < HEAD_DIM
    qi = lax.broadcasted_iota(jnp.int32, (2 * BLOCK, 2 * BLOCK), 0) % BLOCK
    si = lax.broadcasted_iota(jnp.int32, (2 * BLOCK, 2 * BLOCK), 1)
    diff = BLOCK + qi - si
    band = (diff >= 0) & (diff < WINDOW)
    first_pair_rows = lax.broadcasted_iota(jnp.int32, (2 * BLOCK, 1), 0) < BLOCK

    def scores(j, qb):
        r0 = qb * BLOCK
        q2 = jnp.concatenate(
            [q_ref[r0:r0 + BLOCK, (2 * j + pp) * LANES:(2 * j + pp + 1) * LANES] for pp in range(2)],
            axis=0)
        mask = band & ((si >= BLOCK) | (t_idx * (ts // BLOCK) + qb > 0))
        halves = []
        for second in range(2):
            slab = slice((2 * j + second) * LANES, (2 * j + second + 1) * LANES)
            head = 4 * j + second
            sink = jnp.where(first_pair_rows, sinks_ref[head] * LOG2E, sinks_ref[head + 2] * LOG2E)
            s = lax.dot_general(q2, kall_ref[r0:r0 + 2 * BLOCK, slab], (((1,), (1,)), ((), ())),
                                preferred_element_type=f32)
            s = jnp.where(mask, s, MASK_VALUE)
            m = jnp.maximum(jnp.max(s, axis=1, keepdims=True), sink)
            halves.append((s, m, sink))
        return halves

    def weighted_values(j, qb, halves):
        r0 = qb * BLOCK
        parts = []
        for second, (s, m, sink) in enumerate(halves):
            slab = slice((2 * j + second) * LANES, (2 * j + second + 1) * LANES)
            e = jnp.exp2(s - m)
            denom = jnp.sum(e, axis=1, keepdims=True) + jnp.exp2(sink - m)
            parts.append((jnp.dot(e.astype(bf16), vall_ref[r0:r0 + 2 * BLOCK, slab],
                                  preferred_element_type=f32), 1.0 / denom))
        (pv0, inv0), (pv1, inv1) = parts
        out = ((pv0 + pv1) * jnp.where(first_head2, inv0, inv1)).astype(bf16)
        for pp in range(2):
            lo = (2 * j + pp) * LANES
            yattn_ref[r0:r0 + BLOCK, lo:lo + LANES] = out[pp * BLOCK:(pp + 1) * BLOCK, :]

    def lru_projection(c):
        p_lru = jnp.dot(ylru_ref[...], w_proj_ref[:, P_LRU + c * GROUP_W:P_LRU + (c + 1) * GROUP_W],
                        preferred_element_type=f32)
        for half in range(GROUP_W // LANES):
            lg = c * (GROUP_W // LANES) + half
            for i in range(SEG_LEN):
                for grp in range(seg_rows // SUBLANES):
                    row = i * seg_rows + grp * SUBLANES
                    plru_ref.at[lg][pl.ds(SEG_LEN * SUBLANES * grp + i, SUBLANES, stride=SEG_LEN), :] = (
                        p_lru[row:row + SUBLANES, half * LANES:(half + 1) * LANES])

    blocks = [(j, qb) for j in range(N_KV_HEADS) for qb in range(ts // BLOCK)]
    pending = None
    for n, blk in enumerate(blocks + [None]):
        prepared = (blk, scores(*blk)) if blk is not None else None
        if pending is not None:
            weighted_values(*pending[0], pending[1])
        pending = prepared
        if blk is not None and (n + 1) % (len(blocks) // GROUPS) == 0:
            lru_projection((n + 1) // (len(blocks) // GROUPS) - 1)


    kall_ref[0:BLOCK, :] = kall_ref[ts:ts + BLOCK, :]
    vall_ref[0:BLOCK, :] = vall_ref[ts:ts + BLOCK, :]

    for n in range(GROUPS):
        cn = slice(n * GROUP_W, (n + 1) * GROUP_W)
        p_attn = jnp.dot(yattn_ref[...], w_proj_ref[:, P_ATTN + n * GROUP_W:P_ATTN + (n + 1) * GROUP_W],
                         preferred_element_type=f32)
        p_lru = jnp.concatenate([plru_ref[n * (GROUP_W // LANES) + half] for half in range(GROUP_W // LANES)],
                                axis=1)
        merged_ref[:, cn] = (sgl_ref[:, cn] * p_lru + sga_ref[:, cn] * p_attn).astype(bf16)
    m_out = [jnp.dot(merged_ref[...], w_proj_ref[:, P_OUT + n * GROUP_W:P_OUT + (n + 1) * GROUP_W],
                     preferred_element_type=f32) for n in range(GROUPS)]
    mean_sq = sum(jnp.sum(mo * mo, axis=1, keepdims=True) for mo in m_out) * (1.0 / D_MODEL)
    inv_rms = lax.rsqrt(mean_sq + NORM_EPS)
    for n in range(GROUPS):
        cn = slice(n * GROUP_W, (n + 1) * GROUP_W)
        o_ref[:, cn] = h_ref[:, cn] + m_out[n] * inv_rms * post_g_ref[:, cn]


def _mixer(h3d, sinks, cos_t, sin_t, pre_g, w_in_p, conv_w, conv_b, w_gate, b_a, b_x, lam,
           w_proj, post_g):
    bsz, s, _ = h3d.shape
    ts = MIX_TOKENS
    tile = pl.BlockSpec((None, ts, D_MODEL), lambda b, t: (b, t, 0))
    rope_spec = pl.BlockSpec((ts, LANES), lambda b, t: (t, 0))
    return pl.pallas_call(
        _mixer_kernel,
        name="mixer",
        grid=(bsz, s // ts),
        in_specs=[
            pl.BlockSpec(memory_space=pltpu.SMEM),
            tile, rope_spec, rope_spec,
            _const_spec((1, D_MODEL)),
            _const_spec((D_MODEL, IN_WIDTH)),
            _const_spec((CONV_WIDTH, LRU_WIDTH)),
            _const_spec((1, LRU_WIDTH)),
            _const_spec((LRU_WIDTH // MXU_DIM, MXU_DIM, 2 * MXU_DIM)),
            _const_spec((1, LRU_WIDTH)),
            _const_spec((1, LRU_WIDTH)),
            _const_spec((1, LRU_WIDTH)),
            _const_spec((D_MODEL, 3 * D_MODEL + PAD_COLS)),
            _const_spec((1, D_MODEL)),
        ],
        out_specs=tile,
        out_shape=jax.ShapeDtypeStruct(h3d.shape, jnp.float32),
        scratch_shapes=[
            pltpu.VMEM((ts, D_MODEL), jnp.bfloat16),
            pltpu.VMEM((D_MODEL // LANES, ts, LANES), jnp.float32),
            pltpu.VMEM((ts, D_MODEL), jnp.bfloat16),
            pltpu.VMEM((CONV_WIDTH - 1, 1, LRU_WIDTH), jnp.float32),
            pltpu.VMEM((1, LRU_WIDTH), jnp.float32),
            pltpu.VMEM((ts, ATTN_WIDTH), jnp.bfloat16),
            pltpu.VMEM((BLOCK + ts, KV_SLAB_WIDTH), jnp.bfloat16),
            pltpu.VMEM((BLOCK + ts, KV_SLAB_WIDTH), jnp.bfloat16),
            pltpu.VMEM((ts, LRU_WIDTH), jnp.bfloat16),
            pltpu.VMEM((ts, ATTN_WIDTH), jnp.bfloat16),
            pltpu.VMEM((ts, D_MODEL), jnp.float32),
            pltpu.VMEM((ts, D_MODEL), jnp.float32),
            pltpu.VMEM((D_MODEL // LANES, ts, LANES), jnp.float32),
            pltpu.VMEM((ts, D_MODEL), jnp.bfloat16),
        ],
        compiler_params=pltpu.CompilerParams(
            dimension_semantics=("arbitrary", "arbitrary"), vmem_limit_bytes=VMEM_LIMIT_BYTES),
    )(sinks, h3d, cos_t, sin_t, pre_g, w_in_p, conv_w, conv_b, w_gate, b_a, b_x, lam,
      w_proj, post_g)


def _pad_cast(*ws):
    zeros = jnp.zeros((ws[0].shape[0], PAD_COLS), jnp.bfloat16)
    return jnp.concatenate([w.astype(jnp.bfloat16) for w in ws] + [zeros], axis=1)


def _gate_weights(w_a, w_x):
    per = MXU_DIM // LRU_BLOCK_W
    eye = jnp.eye(per, dtype=w_a.dtype)

    def expand(w):
        w4 = w.reshape(LRU_BLOCKS // per, per, LRU_BLOCK_W, LRU_BLOCK_W)
        return jnp.einsum("gncd,nm->gncmd", w4, eye).reshape(LRU_BLOCKS // per, MXU_DIM, MXU_DIM)

    return jnp.concatenate([expand(w_a), expand(w_x)], axis=2).astype(jnp.bfloat16)


def _rope_tables(seq_len):
    inv_freq = ROPE_THETA ** (-jnp.arange(HALF, dtype=jnp.float32) / HALF)
    ang = jnp.arange(seq_len, dtype=jnp.float32)[:, None] * inv_freq[None, :]
    cos, sin = jnp.cos(ang), jnp.sin(ang)
    return (jnp.concatenate([cos, cos, cos, cos], axis=1),
            jnp.concatenate([-sin, sin, -sin, sin], axis=1))


def kernel(x, ffn1_pre_g, ffn1_w_gu, ffn1_w_down, ffn1_post_g, mix_pre_g, w_in, conv_w, conv_b, lru_w_a, lru_b_a, lru_w_x, lru_b_x, lru_lambda, attn_sinks, w_proj_lru, w_proj_attn, w_out, mix_post_g, ffn2_pre_g, ffn2_w_gu, ffn2_w_down, ffn2_post_g):
    bsz, s, d = x.shape
    depth = ffn1_pre_g.shape[0]
    cos_t, sin_t = _rope_tables(s)
    bf = jnp.bfloat16
    h = x
    for l in range(depth):
        h = _ffn(h.reshape(bsz * s, d), ffn1_pre_g[l][None], ffn1_w_gu[l].astype(bf),
                 _pad_cast(ffn1_w_down[l]), ffn1_post_g[l][None]).reshape(bsz, s, d)
        h = _mixer(h, attn_sinks[l], cos_t, sin_t, mix_pre_g[l][None], w_in[l].astype(bf),
                   conv_w[l], conv_b[l][None], _gate_weights(lru_w_a[l], lru_w_x[l]),
                   lru_b_a[l][None], lru_b_x[l][None], lru_lambda[l][None],
                   _pad_cast(w_proj_lru[l], w_proj_attn[l], w_out[l]),
                   mix_post_g[l][None])
        h = _ffn(h.reshape(bsz * s, d), ffn2_pre_g[l][None], ffn2_w_gu[l].astype(bf),
                 _pad_cast(ffn2_w_down[l]), ffn2_post_g[l][None]).reshape(bsz, s, d)
    return h
```

```python
import jax
import jax.numpy as jnp
from jax import lax
from jax.experimental import pallas as pl
from jax.experimental.pallas import tpu as pltpu

D_MODEL = 1024
LRU_WIDTH = 1024
LRU_BLOCKS = 16
LRU_BLOCK_W = LRU_WIDTH // LRU_BLOCKS
CONV_WIDTH = 4
LRU_C = 8.0
HEAD_DIM = 64
HALF = HEAD_DIM // 2
N_Q_HEADS = 16
N_KV_HEADS = 4
ATTN_WIDTH = N_Q_HEADS * HEAD_DIM
KV_WIDTH = N_KV_HEADS * HEAD_DIM
WINDOW = 128
BLOCK = 128
ROPE_THETA = 10000.0
D_FF = 2816
MACARON_SCALE = 0.5
NORM_EPS = 1e-6
MASK_VALUE = -1e30

LANES = 128
SUBLANES = 8
MXU_DIM = 256
VMEM_LIMIT_BYTES = 56 * 1024 * 1024

FFN_TOKENS = 512
FFN_CHUNK = 256
MIX_TOKENS = 512
SEG_LEN = SUBLANES

GROUP_W = 2 * MXU_DIM
GROUPS = LRU_WIDTH // GROUP_W
C_GATE = 0
C_X = C_GATE + LRU_WIDTH
C_Q = C_X + LRU_WIDTH
C_K = C_Q + ATTN_WIDTH
C_V = C_K + KV_WIDTH
C_GL = C_V + KV_WIDTH
C_GA = C_GL + D_MODEL
IN_WIDTH = C_GA + D_MODEL
LOG2E = 1.4426950408889634
P_LRU, P_ATTN, P_OUT = 0, D_MODEL, 2 * D_MODEL
KV_SLAB_WIDTH = N_KV_HEADS * 2 * LANES
SUM_LANE_LOW, SUM_LANE_HIGH = HEAD_DIM, 0
PAD_COLS = LANES


def _rms_norm(x, g):
    return x * lax.rsqrt(jnp.mean(x * x, axis=-1, keepdims=True) + NORM_EPS) * g


def _const_spec(shape):
    n = len(shape)
    return pl.BlockSpec(shape, lambda *_: (0,) * n, pipeline_mode=pl.Buffered(1))


def _ffn_kernel(h_ref, pre_g_ref, w_gu_ref, w_down_ref, post_g_ref, o_ref, act_ref):
    h = h_ref[...]
    xn = _rms_norm(h, pre_g_ref[...]).astype(jnp.bfloat16)
    for c in range(D_FF // FFN_CHUNK):
        lo = c * FFN_CHUNK
        g = jnp.dot(xn, w_gu_ref[:, lo:lo + FFN_CHUNK], preferred_element_type=jnp.float32)
        u = jnp.dot(xn, w_gu_ref[:, D_FF + lo:D_FF + lo + FFN_CHUNK],
                    preferred_element_type=jnp.float32)
        act_ref[:, lo:lo + FFN_CHUNK] = (jax.nn.silu(g) * u).astype(jnp.bfloat16)
    f = [jnp.dot(act_ref[...], w_down_ref[:, n * GROUP_W:(n + 1) * GROUP_W],
                 preferred_element_type=jnp.float32) for n in range(D_MODEL // GROUP_W)]
    mean_sq = sum(jnp.sum(fn * fn, axis=1, keepdims=True) for fn in f) * (1.0 / D_MODEL)
    scale = MACARON_SCALE * lax.rsqrt(mean_sq + NORM_EPS)
    for n in range(D_MODEL // GROUP_W):
        cn = slice(n * GROUP_W, (n + 1) * GROUP_W)
        o_ref[:, cn] = h_ref[:, cn] + f[n] * scale * post_g_ref[:, cn]


def _ffn(h2d, pre_g, w_gu, w_down, post_g):
    m = h2d.shape[0]
    return pl.pallas_call(
        _ffn_kernel,
        name="ffn",
        grid=(m // FFN_TOKENS,),
        in_specs=[
            pl.BlockSpec((FFN_TOKENS, D_MODEL), lambda i: (i, 0)),
            _const_spec((1, D_MODEL)),
            _const_spec((D_MODEL, 2 * D_FF)),
            _const_spec((D_FF, D_MODEL + PAD_COLS)),
            _const_spec((1, D_MODEL)),
        ],
        out_specs=pl.BlockSpec((FFN_TOKENS, D_MODEL), lambda i: (i, 0)),
        out_shape=jax.ShapeDtypeStruct((m, D_MODEL), jnp.float32),
        scratch_shapes=[pltpu.VMEM((FFN_TOKENS, D_FF), jnp.bfloat16)],
        compiler_params=pltpu.CompilerParams(
            dimension_semantics=("arbitrary",), vmem_limit_bytes=VMEM_LIMIT_BYTES),
    )(h2d, pre_g, w_gu, w_down, post_g)


def _mixer_kernel(sinks_ref, h_ref, cos_ref, sin_ref, pre_g_ref, w_in_ref, conv_w_ref,
                  conv_b_ref, w_gate_ref, b_a_ref, b_x_ref, lam_ref, w_proj_ref, post_g_ref, o_ref,
                  un_ref, unf_ref, unp_ref, xhist_ref, hstate_ref, q_ref, kall_ref, vall_ref, ylru_ref,
                  yattn_ref, sgl_ref, sga_ref, plru_ref, merged_ref):
    ts = MIX_TOKENS
    seg_rows = ts // SEG_LEN
    t_idx = pl.program_id(1)
    f32, bf16 = jnp.float32, jnp.bfloat16

    @pl.when(t_idx == 0)
    def _():
        xhist_ref[...] = jnp.zeros_like(xhist_ref)
        hstate_ref[...] = jnp.zeros_like(hstate_ref)
        kall_ref[0:BLOCK, :] = jnp.zeros((BLOCK, KV_SLAB_WIDTH), bf16)
        vall_ref[0:BLOCK, :] = jnp.zeros((BLOCK, KV_SLAB_WIDTH), bf16)

    un = _rms_norm(h_ref[...], pre_g_ref[...])
    un_ref[...] = un.astype(bf16)
    for lg in range(D_MODEL // LANES):
        unf_ref[lg] = un[:, lg * LANES:(lg + 1) * LANES]
    for lg in range(D_MODEL // LANES):
        pieces = [unf_ref.at[lg][pl.ds(SEG_LEN * SUBLANES * grp + i, SUBLANES, stride=SEG_LEN), :]
                  for i in range(SEG_LEN) for grp in range(seg_rows // SUBLANES)]
        unp_ref[:, lg * LANES:(lg + 1) * LANES] = jnp.concatenate(pieces, axis=0).astype(bf16)

    def proj(lo, src_ref=un_ref):
        return jnp.dot(src_ref[...], w_in_ref[:, lo:lo + GROUP_W], preferred_element_type=f32)

    lane = lax.broadcasted_iota(jnp.int32, (ts, LANES), 1)
    first_half = (lane % HEAD_DIM) < HALF
    first_head = lane < HEAD_DIM

    def rope(x, c, s):
        partner = jnp.where(first_half, pltpu.roll(x, LANES - HALF, axis=1),
                            pltpu.roll(x, HALF, axis=1))
        return x * c + partner * s

    cos = cos_ref[...]
    sin = sin_ref[...]
    q_scale = (HEAD_DIM ** -0.5) * LOG2E
    cos_q = cos * q_scale
    sin_q = sin * q_scale
    seg_row = lax.broadcasted_iota(jnp.int32, (seg_rows, GROUP_W), 0)

    def shift_rows(cur, first_row):
        return jnp.where(seg_row == 0, first_row, pltpu.roll(cur, 1, axis=0))

    kv_nat = proj(C_K)
    k_nat, v_nat = kv_nat[:, :KV_WIDTH], kv_nat[:, KV_WIDTH:]
    for jj in range(N_KV_HEADS // 2):
        kr = rope(k_nat[:, jj * LANES:(jj + 1) * LANES], cos, sin)
        vv = v_nat[:, jj * LANES:(jj + 1) * LANES]
        kr_swapped = pltpu.roll(kr, HEAD_DIM, axis=1)
        vv_swapped = pltpu.roll(vv, HEAD_DIM, axis=1)
        for half in range(2):
            j = 2 * jj + half
            for dst_ref, same, swapped, fill_low, fill_high in (
                    (kall_ref, kr, kr_swapped, 0.0, 0.0),
                    (vall_ref, vv, vv_swapped, (lane == SUM_LANE_LOW).astype(f32),
                     (lane == SUM_LANE_HIGH).astype(f32))):
                low, high = (same, swapped) if half == 0 else (swapped, same)
                dst_ref[BLOCK:BLOCK + ts, (2 * j) * LANES:(2 * j + 1) * LANES] = (
                    jnp.where(first_head, low, fill_low).astype(bf16))
                dst_ref[BLOCK:BLOCK + ts, (2 * j + 1) * LANES:(2 * j + 2) * LANES] = (
                    jnp.where(first_head, fill_high, high).astype(bf16))

    for g in range(GROUPS):
        qg = proj(C_Q + g * GROUP_W)
        for pp in range(GROUP_W // LANES):
            lo = g * GROUP_W + pp * LANES
            q_ref[:, lo:lo + LANES] = rope(qg[:, pp * LANES:(pp + 1) * LANES], cos_q, sin_q).astype(bf16)

    def lru_stages(g):
        cg = slice(g * GROUP_W, (g + 1) * GROUP_W)
        st = {}

        def conv_and_gates():
            xg = proj(C_X + g * GROUP_W, unp_ref)
            x_blk = [xg[i * seg_rows:(i + 1) * seg_rows, :] for i in range(SEG_LEN)]
            wrapped = {}
            for i_src in range(SEG_LEN - CONV_WIDTH + 1, SEG_LEN):
                k = i_src - (SEG_LEN - CONV_WIDTH + 1)
                wrapped[i_src] = shift_rows(x_blk[i_src], xhist_ref[k, :, cg])
                xhist_ref[k, :, cg] = x_blk[i_src][seg_rows - 1:seg_rows, :]
            xc_blk = []
            for i in range(SEG_LEN):
                acc = conv_b_ref[:, cg]
                for d in range(CONV_WIDTH):
                    src = x_blk[i - d] if i >= d else wrapped[i - d + SEG_LEN]
                    acc = acc + src * conv_w_ref[CONV_WIDTH - 1 - d:CONV_WIDTH - d, cg]
                xc_blk.append(acc)
            st["xc"] = jnp.concatenate(xc_blk, axis=0)
            xc_b = st["xc"].astype(bf16)
            st["gates"] = [
                jnp.dot(xc_b[:, k * MXU_DIM:(k + 1) * MXU_DIM], w_gate_ref[g * (GROUP_W // MXU_DIM) + k],
                        preferred_element_type=f32) for k in range(GROUP_W // MXU_DIM)]

        def decay_and_input():
            gates, xc = st["gates"], st["xc"]
            r = jax.nn.sigmoid(jnp.concatenate([ri[:, :MXU_DIM] for ri in gates], axis=1) + b_a_ref[:, cg])
            i_gate = jax.nn.sigmoid(jnp.concatenate([ri[:, MXU_DIM:] for ri in gates], axis=1)
                                    + b_x_ref[:, cg])
            lam = lam_ref[:, cg]
            softplus_neg_lam = jnp.maximum(-lam, 0.0) + jnp.log1p(jnp.exp(-jnp.abs(lam)))
            log_a = (-LRU_C) * r * softplus_neg_lam
            a = jnp.exp(log_a)
            t = -jnp.tanh(log_a) * (a * a + 1.0)
            mult = jnp.where(t > 0.0, t * lax.rsqrt(t), 0.0)
            st["a"], st["b"] = a, mult * (i_gate * xc)

        def local_scan():
            a, b = st["a"], st["b"]
            a_blk = [a[i * seg_rows:(i + 1) * seg_rows, :] for i in range(SEG_LEN)]
            b_blk = [b[i * seg_rows:(i + 1) * seg_rows, :] for i in range(SEG_LEN)]
            h_loc, a_cum = [b_blk[0]], [a_blk[0]]
            for i in range(1, SEG_LEN):
                h_loc.append(a_blk[i] * h_loc[-1] + b_blk[i])
                a_cum.append(a_blk[i] * a_cum[-1])
            st["h_loc"], st["a_cum"] = h_loc, a_cum

        def carry_and_gate():
            h_loc, a_cum = st["h_loc"], st["a_cum"]
            seg_a, seg_h = a_cum[-1], h_loc[-1]
            shift = 1
            while shift < SUBLANES:
                keep = (seg_row % SUBLANES) >= shift
                a_prev = jnp.where(keep, pltpu.roll(seg_a, shift, axis=0), 1.0)
                h_prev = jnp.where(keep, pltpu.roll(seg_h, shift, axis=0), 0.0)
                seg_h = seg_a * h_prev + seg_h
                seg_a = seg_a * a_prev
                shift *= 2
            carry = hstate_ref[:, cg]
            state_in = carry
            ends = []
            for v in range(seg_rows // SUBLANES):
                lo = v * SUBLANES
                hv = seg_a[lo:lo + SUBLANES, :] * carry + seg_h[lo:lo + SUBLANES, :]
                ends.append(hv)
                carry = hv[SUBLANES - 1:SUBLANES, :]
            hstate_ref[:, cg] = carry
            seg_in = shift_rows(jnp.concatenate(ends, axis=0), state_in)
            h_lru = jnp.concatenate([h_loc[i] + a_cum[i] * seg_in for i in range(SEG_LEN)], axis=0)
            ylru_ref[:, cg] = (h_lru * jax.nn.gelu(proj(C_GATE + g * GROUP_W, unp_ref))).astype(bf16)

        def merge_gates():
            sgl_ref[:, cg] = jax.nn.sigmoid(proj(C_GL + g * GROUP_W))
            sga_ref[:, cg] = jax.nn.sigmoid(proj(C_GA + g * GROUP_W))

        return [conv_and_gates, decay_and_input, local_scan, carry_and_gate, merge_gates]

    first_head2 = lax.broadcasted_iota(jnp.int32, (2 * BLOCK, LANES), 1) < HEAD_DIM
    qi = lax.broadcasted_iota(jnp.int32, (2 * BLOCK, 2 * BLOCK), 0) % BLOCK
    si = lax.broadcasted_iota(jnp.int32, (2 * BLOCK, 2 * BLOCK), 1)
    diff = BLOCK + qi - si
    band = (diff >= 0) & (diff < WINDOW)
    first_pair_rows = lax.broadcasted_iota(jnp.int32, (2 * BLOCK, 1), 0) < BLOCK

    def scores(j, qb):
        r0 = qb * BLOCK
        q2 = jnp.concatenate(
            [q_ref[r0:r0 + BLOCK, (2 * j + pp) * LANES:(2 * j + pp + 1) * LANES] for pp in range(2)],
            axis=0)
        mask = band & ((si >= BLOCK) | (t_idx * (ts // BLOCK) + qb > 0))
        halves = []
        for second in range(2):
            slab = slice((2 * j + second) * LANES, (2 * j + second + 1) * LANES)
            head = 4 * j + second
            sink = jnp.where(first_pair_rows, sinks_ref[head] * LOG2E, sinks_ref[head + 2] * LOG2E)
            s = lax.dot_general(q2, kall_ref[r0:r0 + 2 * BLOCK, slab], (((1,), (1,)), ((), ())),
                                preferred_element_type=f32)
            s = jnp.where(mask, s, MASK_VALUE)
            m = jnp.maximum(jnp.max(s, axis=1, keepdims=True), sink)
            halves.append((s, m, sink))
        return halves

    def weighted_values(j, qb, halves):
        r0 = qb * BLOCK
        parts = []
        for second, (s, m, sink) in enumerate(halves):
            slab = slice((2 * j + second) * LANES, (2 * j + second + 1) * LANES)
            e = jnp.exp2(s - m).astype(bf16)
            pv = jnp.dot(e, vall_ref[r0:r0 + 2 * BLOCK, slab], preferred_element_type=f32)
            sum_lane = SUM_LANE_HIGH if second else SUM_LANE_LOW
            denom = pv[:, sum_lane:sum_lane + 1] + jnp.exp2(sink - m)
            parts.append(pv * (1.0 / denom))
        out = jnp.where(first_head2, parts[0], parts[1]).astype(bf16)
        for pp in range(2):
            lo = (2 * j + pp) * LANES
            yattn_ref[r0:r0 + BLOCK, lo:lo + LANES] = out[pp * BLOCK:(pp + 1) * BLOCK, :]

    def attention_stages():
        blocks = [(j, qb) for j in range(N_KV_HEADS) for qb in range(ts // BLOCK)]
        pending = {}

        def stage(n):
            def run():
                if n < len(blocks):
                    pending[n] = scores(*blocks[n])
                if n >= 1:
                    weighted_values(*blocks[n - 1], pending.pop(n - 1))
            return run

        return [stage(n) for n in range(len(blocks) + 1)]

    def lru_projection(c):
        p_lru = jnp.dot(ylru_ref[...], w_proj_ref[:, P_LRU + c * GROUP_W:P_LRU + (c + 1) * GROUP_W],
                        preferred_element_type=f32)
        for half in range(GROUP_W // LANES):
            lg = c * (GROUP_W // LANES) + half
            for i in range(SEG_LEN):
                for grp in range(seg_rows // SUBLANES):
                    row = i * seg_rows + grp * SUBLANES
                    plru_ref.at[lg][pl.ds(SEG_LEN * SUBLANES * grp + i, SUBLANES, stride=SEG_LEN), :] = (
                        p_lru[row:row + SUBLANES, half * LANES:(half + 1) * LANES])

    lru_list = [stage for g in range(GROUPS) for stage in lru_stages(g)]
    lru_list += [lambda c=c: lru_projection(c) for c in range(GROUPS)]
    att_list = attention_stages()
    att_per_lru = -(-len(att_list) // len(lru_list))
    while lru_list or att_list:
        if lru_list:
            lru_list.pop(0)()
        for _ in range(att_per_lru):
            if att_list:
                att_list.pop(0)()

    kall_ref[0:BLOCK, :] = kall_ref[ts:ts + BLOCK, :]
    vall_ref[0:BLOCK, :] = vall_ref[ts:ts + BLOCK, :]

    for n in range(GROUPS):
        cn = slice(n * GROUP_W, (n + 1) * GROUP_W)
        p_attn = jnp.dot(yattn_ref[...], w_proj_ref[:, P_ATTN + n * GROUP_W:P_ATTN + (n + 1) * GROUP_W],
                         preferred_element_type=f32)
        p_lru = jnp.concatenate([plru_ref[n * (GROUP_W // LANES) + half] for half in range(GROUP_W // LANES)],
                                axis=1)
        merged_ref[:, cn] = (sgl_ref[:, cn] * p_lru + sga_ref[:, cn] * p_attn).astype(bf16)
    m_out = [jnp.dot(merged_ref[...], w_proj_ref[:, P_OUT + n * GROUP_W:P_OUT + (n + 1) * GROUP_W],
                     preferred_element_type=f32) for n in range(GROUPS)]
    mean_sq = sum(jnp.sum(mo * mo, axis=1, keepdims=True) for mo in m_out) * (1.0 / D_MODEL)
    inv_rms = lax.rsqrt(mean_sq + NORM_EPS)
    for n in range(GROUPS):
        cn = slice(n * GROUP_W, (n + 1) * GROUP_W)
        o_ref[:, cn] = h_ref[:, cn] + m_out[n] * inv_rms * post_g_ref[:, cn]


def _mixer(h3d, sinks, cos_t, sin_t, pre_g, w_in_p, conv_w, conv_b, w_gate, b_a, b_x, lam,
           w_proj, post_g):
    bsz, s, _ = h3d.shape
    ts = MIX_TOKENS
    tile = pl.BlockSpec((None, ts, D_MODEL), lambda b, t: (b, t, 0))
    rope_spec = pl.BlockSpec((ts, LANES), lambda b, t: (t, 0))
    return pl.pallas_call(
        _mixer_kernel,
        name="mixer",
        grid=(bsz, s // ts),
        in_specs=[
            pl.BlockSpec(memory_space=pltpu.SMEM),
            tile, rope_spec, rope_spec,
            _const_spec((1, D_MODEL)),
            _const_spec((D_MODEL, IN_WIDTH)),
            _const_spec((CONV_WIDTH, LRU_WIDTH)),
            _const_spec((1, LRU_WIDTH)),
            _const_spec((LRU_WIDTH // MXU_DIM, MXU_DIM, 2 * MXU_DIM)),
            _const_spec((1, LRU_WIDTH)),
            _const_spec((1, LRU_WIDTH)),
            _const_spec((1, LRU_WIDTH)),
            _const_spec((D_MODEL, 3 * D_MODEL + PAD_COLS)),
            _const_spec((1, D_MODEL)),
        ],
        out_specs=tile,
        out_shape=jax.ShapeDtypeStruct(h3d.shape, jnp.float32),
        scratch_shapes=[
            pltpu.VMEM((ts, D_MODEL), jnp.bfloat16),
            pltpu.VMEM((D_MODEL // LANES, ts, LANES), jnp.float32),
            pltpu.VMEM((ts, D_MODEL), jnp.bfloat16),
            pltpu.VMEM((CONV_WIDTH - 1, 1, LRU_WIDTH), jnp.float32),
            pltpu.VMEM((1, LRU_WIDTH), jnp.float32),
            pltpu.VMEM((ts, ATTN_WIDTH), jnp.bfloat16),
            pltpu.VMEM((BLOCK + ts, KV_SLAB_WIDTH), jnp.bfloat16),
            pltpu.VMEM((BLOCK + ts, KV_SLAB_WIDTH), jnp.bfloat16),
            pltpu.VMEM((ts, LRU_WIDTH), jnp.bfloat16),
            pltpu.VMEM((ts, ATTN_WIDTH), jnp.bfloat16),
            pltpu.VMEM((ts, D_MODEL), jnp.float32),
            pltpu.VMEM((ts, D_MODEL), jnp.float32),
            pltpu.VMEM((D_MODEL // LANES, ts, LANES), jnp.float32),
            pltpu.VMEM((ts, D_MODEL), jnp.bfloat16),
        ],
        compiler_params=pltpu.CompilerParams(
            dimension_semantics=("arbitrary", "arbitrary"), vmem_limit_bytes=VMEM_LIMIT_BYTES),
    )(sinks, h3d, cos_t, sin_t, pre_g, w_in_p, conv_w, conv_b, w_gate, b_a, b_x, lam,
      w_proj, post_g)


def _pad_cast(*ws):
    zeros = jnp.zeros((ws[0].shape[0], PAD_COLS), jnp.bfloat16)
    return jnp.concatenate([w.astype(jnp.bfloat16) for w in ws] + [zeros], axis=1)


def _gate_weights(w_a, w_x):
    per = MXU_DIM // LRU_BLOCK_W
    eye = jnp.eye(per, dtype=w_a.dtype)

    def expand(w):
        w4 = w.reshape(LRU_BLOCKS // per, per, LRU_BLOCK_W, LRU_BLOCK_W)
        return jnp.einsum("gncd,nm->gncmd", w4, eye).reshape(LRU_BLOCKS // per, MXU_DIM, MXU_DIM)

    return jnp.concatenate([expand(w_a), expand(w_x)], axis=2).astype(jnp.bfloat16)


def _rope_tables(seq_len):
    inv_freq = ROPE_THETA ** (-jnp.arange(HALF, dtype=jnp.float32) / HALF)
    ang = jnp.arange(seq_len, dtype=jnp.float32)[:, None] * inv_freq[None, :]
    cos, sin = jnp.cos(ang), jnp.sin(ang)
    return (jnp.concatenate([cos, cos, cos, cos], axis=1),
            jnp.concatenate([-sin, sin, -sin, sin], axis=1))


def kernel(x, ffn1_pre_g, ffn1_w_gu, ffn1_w_down, ffn1_post_g, mix_pre_g, w_in, conv_w, conv_b, lru_w_a, lru_b_a, lru_w_x, lru_b_x, lru_lambda, attn_sinks, w_proj_lru, w_proj_attn, w_out, mix_post_g, ffn2_pre_g, ffn2_w_gu, ffn2_w_down, ffn2_post_g):
    bsz, s, d = x.shape
    depth = ffn1_pre_g.shape[0]
    cos_t, sin_t = _rope_tables(s)
    bf = jnp.bfloat16
    h = x
    for l in range(depth):
        h = _ffn(h.reshape(bsz * s, d), ffn1_pre_g[l][None], ffn1_w_gu[l].astype(bf),
                 _pad_cast(ffn1_w_down[l]), ffn1_post_g[l][None]).reshape(bsz, s, d)
        h = _mixer(h, attn_sinks[l], cos_t, sin_t, mix_pre_g[l][None], w_in[l].astype(bf),
                   conv_w[l], conv_b[l][None], _gate_weights(lru_w_a[l], lru_w_x[l]),
                   lru_b_a[l][None], lru_b_x[l][None], lru_lambda[l][None],
                   _pad_cast(w_proj_lru[l], w_proj_attn[l], w_out[l]),
                   mix_post_g[l][None])
        h = _ffn(h.reshape(bsz * s, d), ffn2_pre_g[l][None], ffn2_w_gu[l].astype(bf),
                 _pad_cast(ffn2_w_down[l]), ffn2_post_g[l][None]).reshape(bsz, s, d)
    return h
```

```python
import jax
import jax.numpy as jnp
from jax import lax
from jax.experimental import pallas as pl
from jax.experimental.pallas import tpu as pltpu

D_MODEL = 1024
LRU_WIDTH = 1024
LRU_BLOCKS = 16
LRU_BLOCK_W = LRU_WIDTH // LRU_BLOCKS
CONV_WIDTH = 4
LRU_C = 8.0
HEAD_DIM = 64
HALF = HEAD_DIM // 2
N_Q_HEADS = 16
N_KV_HEADS = 4
ATTN_WIDTH = N_Q_HEADS * HEAD_DIM
KV_WIDTH = N_KV_HEADS * HEAD_DIM
WINDOW = 128
BLOCK = 128
ROPE_THETA = 10000.0
D_FF = 2816
MACARON_SCALE = 0.5
NORM_EPS = 1e-6
MASK_VALUE = -1e30

LANES = 128
SUBLANES = 8
MXU_DIM = 256
VMEM_LIMIT_BYTES = 56 * 1024 * 1024

FFN_TOKENS = 512
FFN_CHUNK = 256
FFN_PIECE = 64
MIX_TOKENS = 512
SEG_LEN = SUBLANES

GROUP_W = 2 * MXU_DIM
GROUPS = LRU_WIDTH // GROUP_W
C_GATE = 0
C_X = C_GATE + LRU_WIDTH
C_Q = C_X + LRU_WIDTH
C_K = C_Q + ATTN_WIDTH
C_V = C_K + KV_WIDTH
C_GL = C_V + KV_WIDTH
C_GA = C_GL + D_MODEL
IN_WIDTH = C_GA + D_MODEL
LOG2E = 1.4426950408889634
P_LRU, P_ATTN, P_OUT = 0, D_MODEL, 2 * D_MODEL
KV_SLAB_WIDTH = N_KV_HEADS * 2 * LANES
SUM_LANE_LOW, SUM_LANE_HIGH = HEAD_DIM, 0
PAD_COLS = LANES


def _rms_norm(x, g):
    return x * lax.rsqrt(jnp.mean(x * x, axis=-1, keepdims=True) + NORM_EPS) * g


def _dependent_zero(values):
    bits = None
    for v in values:
        b = pltpu.bitcast(v[0:SUBLANES, 0:LANES], jnp.uint32)
        bits = b if bits is None else bits | b
    half = jnp.uint32(16)
    zero = pltpu.bitcast(lax.shift_right_logical(lax.shift_right_logical(bits, half), half), jnp.float32)
    return jnp.concatenate([zero, zero], axis=0).astype(jnp.bfloat16)


def _const_spec(shape):
    n = len(shape)
    return pl.BlockSpec(shape, lambda *_: (0,) * n, pipeline_mode=pl.Buffered(1))


def _ffn_kernel(h_ref, h_next_ref, h_prev_ref, pre_g_ref, w_gu_ref, w_down_ref, post_g_ref, o_ref,
                xn_ref, xn_next_ref, act_ref, f_ref):
    i = pl.program_id(0)
    n_tiles = pl.num_programs(0) - 1
    bf16, f32 = jnp.bfloat16, jnp.float32
    n_pieces = FFN_TOKENS // FFN_PIECE
    n_chunks = D_FF // FFN_CHUNK

    def finish_rows(rows):
        out = h_prev_ref[rows, :] + MACARON_SCALE * _rms_norm(f_ref[rows, :], post_g_ref[...])
        o_ref[rows, :] = out
        return out

    @pl.when(i == 0)
    def _():
        xn_ref[...] = _rms_norm(h_ref[...], pre_g_ref[...]).astype(bf16)
        f_ref[...] = jnp.zeros_like(f_ref)

    @pl.when(i < n_tiles)
    def _():
        for c in range(n_chunks):
            lo = c * FFN_CHUNK
            xn = xn_ref[...]
            g = jnp.dot(xn, w_gu_ref[:, lo:lo + FFN_CHUNK], preferred_element_type=f32)
            u = jnp.dot(xn, w_gu_ref[:, D_FF + lo:D_FF + lo + FFN_CHUNK], preferred_element_type=f32)
            act_ref[:, lo:lo + FFN_CHUNK] = (jax.nn.silu(g) * u).astype(bf16)
            anchors = []
            if c < n_pieces:
                anchors.append(finish_rows(slice(c * FFN_PIECE, (c + 1) * FFN_PIECE)))
            p = c - (n_chunks - n_pieces)
            if p >= 0:
                rows = slice(p * FFN_PIECE, (p + 1) * FFN_PIECE)
                xn_piece = _rms_norm(h_next_ref[rows, :], pre_g_ref[...])
                xn_next_ref[rows, :] = xn_piece.astype(bf16)
                anchors.append(xn_piece)
            if c + 1 < n_chunks:
                tile = (slice(0, 2 * SUBLANES), slice(0, LANES))
                xn_ref[tile] = xn_ref[tile] + _dependent_zero(anchors)
        for n in range(D_MODEL // GROUP_W):
            cn = slice(n * GROUP_W, (n + 1) * GROUP_W)
            f_ref[:, cn] = jnp.dot(act_ref[...], w_down_ref[:, cn], preferred_element_type=f32)
        xn_ref[...] = xn_next_ref[...]

    @pl.when(i == n_tiles)
    def _():
        for p in range(n_pieces):
            finish_rows(slice(p * FFN_PIECE, (p + 1) * FFN_PIECE))


def _ffn(h2d, pre_g, w_gu, w_down, post_g):
    m = h2d.shape[0]
    n_tiles = m // FFN_TOKENS
    tile = (FFN_TOKENS, D_MODEL)
    return pl.pallas_call(
        _ffn_kernel,
        name="ffn",
        grid=(n_tiles + 1,),
        in_specs=[
            pl.BlockSpec(tile, lambda i: (jnp.minimum(i, n_tiles - 1), 0)),
            pl.BlockSpec(tile, lambda i: (jnp.minimum(i + 1, n_tiles - 1), 0)),
            pl.BlockSpec(tile, lambda i: (jnp.maximum(i - 1, 0), 0)),
            _const_spec((1, D_MODEL)),
            _const_spec((D_MODEL, 2 * D_FF)),
            _const_spec((D_FF, D_MODEL + PAD_COLS)),
            _const_spec((1, D_MODEL)),
        ],
        out_specs=pl.BlockSpec(tile, lambda i: (jnp.maximum(i - 1, 0), 0)),
        out_shape=jax.ShapeDtypeStruct((m, D_MODEL), jnp.float32),
        scratch_shapes=[
            pltpu.VMEM(tile, jnp.bfloat16),
            pltpu.VMEM(tile, jnp.bfloat16),
            pltpu.VMEM((FFN_TOKENS, D_FF), jnp.bfloat16),
            pltpu.VMEM(tile, jnp.float32),
        ],
        compiler_params=pltpu.CompilerParams(
            dimension_semantics=("arbitrary",), vmem_limit_bytes=VMEM_LIMIT_BYTES),
    )(h2d, h2d, h2d, pre_g, w_gu, w_down, post_g)


def _mixer_kernel(sinks_ref, h_ref, cos_ref, sin_ref, pre_g_ref, w_in_ref, conv_w_ref,
                  conv_b_ref, w_gate_ref, b_a_ref, b_x_ref, lam_ref, w_proj_ref, post_g_ref, o_ref,
                  un_ref, unf_ref, unp_ref, xhist_ref, hstate_ref, q_ref, kall_ref, vall_ref, ylru_ref,
                  yattn_ref, sgl_ref, sga_ref, plru_ref, merged_ref):
    ts = MIX_TOKENS
    seg_rows = ts // SEG_LEN
    t_idx = pl.program_id(1)
    f32, bf16 = jnp.float32, jnp.bfloat16

    @pl.when(t_idx == 0)
    def _():
        xhist_ref[...] = jnp.zeros_like(xhist_ref)
        hstate_ref[...] = jnp.zeros_like(hstate_ref)
        kall_ref[0:BLOCK, :] = jnp.zeros((BLOCK, KV_SLAB_WIDTH), bf16)
        vall_ref[0:BLOCK, :] = jnp.zeros((BLOCK, KV_SLAB_WIDTH), bf16)

    un = _rms_norm(h_ref[...], pre_g_ref[...])
    un_ref[...] = un.astype(bf16)
    for lg in range(D_MODEL // LANES):
        unf_ref[lg] = un[:, lg * LANES:(lg + 1) * LANES]
    for lg in range(D_MODEL // LANES):
        pieces = [unf_ref.at[lg][pl.ds(SEG_LEN * SUBLANES * grp + i, SUBLANES, stride=SEG_LEN), :]
                  for i in range(SEG_LEN) for grp in range(seg_rows // SUBLANES)]
        unp_ref[:, lg * LANES:(lg + 1) * LANES] = jnp.concatenate(pieces, axis=0).astype(bf16)

    def proj(lo, src_ref=un_ref):
        return jnp.dot(src_ref[...], w_in_ref[:, lo:lo + GROUP_W], preferred_element_type=f32)

    lane = lax.broadcasted_iota(jnp.int32, (ts, LANES), 1)
    first_half = (lane % HEAD_DIM) < HALF
    first_head = lane < HEAD_DIM

    def rope(x, c, s):
        partner = jnp.where(first_half, pltpu.roll(x, LANES - HALF, axis=1),
                            pltpu.roll(x, HALF, axis=1))
        return x * c + partner * s

    cos = cos_ref[...]
    sin = sin_ref[...]
    q_scale = (HEAD_DIM ** -0.5) * LOG2E
    cos_q = cos * q_scale
    sin_q = sin * q_scale
    seg_row = lax.broadcasted_iota(jnp.int32, (seg_rows, GROUP_W), 0)

    def shift_rows(cur, first_row):
        return jnp.where(seg_row == 0, first_row, pltpu.roll(cur, 1, axis=0))

    kv_nat = proj(C_K)
    k_nat, v_nat = kv_nat[:, :KV_WIDTH], kv_nat[:, KV_WIDTH:]
    for jj in range(N_KV_HEADS // 2):
        kr = rope(k_nat[:, jj * LANES:(jj + 1) * LANES], cos, sin)
        vv = v_nat[:, jj * LANES:(jj + 1) * LANES]
        kr_swapped = pltpu.roll(kr, HEAD_DIM, axis=1)
        vv_swapped = pltpu.roll(vv, HEAD_DIM, axis=1)
        for half in range(2):
            j = 2 * jj + half
            for dst_ref, same, swapped, fill_low, fill_high in (
                    (kall_ref, kr, kr_swapped, 0.0, 0.0),
                    (vall_ref, vv, vv_swapped, (lane == SUM_LANE_LOW).astype(f32),
                     (lane == SUM_LANE_HIGH).astype(f32))):
                low, high = (same, swapped) if half == 0 else (swapped, same)
                dst_ref[BLOCK:BLOCK + ts, (2 * j) * LANES:(2 * j + 1) * LANES] = (
                    jnp.where(first_head, low, fill_low).astype(bf16))
                dst_ref[BLOCK:BLOCK + ts, (2 * j + 1) * LANES:(2 * j + 2) * LANES] = (
                    jnp.where(first_head, fill_high, high).astype(bf16))

    for g in range(GROUPS):
        qg = proj(C_Q + g * GROUP_W)
        for pp in range(GROUP_W // LANES):
            lo = g * GROUP_W + pp * LANES
            q_ref[:, lo:lo + LANES] = rope(qg[:, pp * LANES:(pp + 1) * LANES], cos_q, sin_q).astype(bf16)

    def lru_stages(g):
        cg = slice(g * GROUP_W, (g + 1) * GROUP_W)
        st = {}

        def conv_and_gates():
            xg = proj(C_X + g * GROUP_W, unp_ref)
            x_blk = [xg[i * seg_rows:(i + 1) * seg_rows, :] for i in range(SEG_LEN)]
            wrapped = {}
            for i_src in range(SEG_LEN - CONV_WIDTH + 1, SEG_LEN):
                k = i_src - (SEG_LEN - CONV_WIDTH + 1)
                wrapped[i_src] = shift_rows(x_blk[i_src], xhist_ref[k, :, cg])
                xhist_ref[k, :, cg] = x_blk[i_src][seg_rows - 1:seg_rows, :]
            xc_blk = []
            for i in range(SEG_LEN):
                acc = conv_b_ref[:, cg]
                for d in range(CONV_WIDTH):
                    src = x_blk[i - d] if i >= d else wrapped[i - d + SEG_LEN]
                    acc = acc + src * conv_w_ref[CONV_WIDTH - 1 - d:CONV_WIDTH - d, cg]
                xc_blk.append(acc)
            st["xc"] = jnp.concatenate(xc_blk, axis=0)
            xc_b = st["xc"].astype(bf16)
            st["gates"] = [
                jnp.dot(xc_b[:, k * MXU_DIM:(k + 1) * MXU_DIM], w_gate_ref[g * (GROUP_W // MXU_DIM) + k],
                        preferred_element_type=f32) for k in range(GROUP_W // MXU_DIM)]

        def decay_and_input():
            gates, xc = st["gates"], st["xc"]
            r = jax.nn.sigmoid(jnp.concatenate([ri[:, :MXU_DIM] for ri in gates], axis=1) + b_a_ref[:, cg])
            i_gate = jax.nn.sigmoid(jnp.concatenate([ri[:, MXU_DIM:] for ri in gates], axis=1)
                                    + b_x_ref[:, cg])
            lam = lam_ref[:, cg]
            softplus_neg_lam = jnp.maximum(-lam, 0.0) + jnp.log1p(jnp.exp(-jnp.abs(lam)))
            log_a = (-LRU_C) * r * softplus_neg_lam
            a = jnp.exp(log_a)
            t = -jnp.tanh(log_a) * (a * a + 1.0)
            mult = jnp.where(t > 0.0, t * lax.rsqrt(t), 0.0)
            st["a"], st["b"] = a, mult * (i_gate * xc)

        def local_scan():
            a, b = st["a"], st["b"]
            a_blk = [a[i * seg_rows:(i + 1) * seg_rows, :] for i in range(SEG_LEN)]
            b_blk = [b[i * seg_rows:(i + 1) * seg_rows, :] for i in range(SEG_LEN)]
            h_loc, a_cum = [b_blk[0]], [a_blk[0]]
            for i in range(1, SEG_LEN):
                h_loc.append(a_blk[i] * h_loc[-1] + b_blk[i])
                a_cum.append(a_blk[i] * a_cum[-1])
            st["h_loc"], st["a_cum"] = h_loc, a_cum

        def carry_and_gate():
            h_loc, a_cum = st["h_loc"], st["a_cum"]
            seg_a, seg_h = a_cum[-1], h_loc[-1]
            shift = 1
            while shift < SUBLANES:
                keep = (seg_row % SUBLANES) >= shift
                a_prev = jnp.where(keep, pltpu.roll(seg_a, shift, axis=0), 1.0)
                h_prev = jnp.where(keep, pltpu.roll(seg_h, shift, axis=0), 0.0)
                seg_h = seg_a * h_prev + seg_h
                seg_a = seg_a * a_prev
                shift *= 2
            carry = hstate_ref[:, cg]
            state_in = carry
            ends = []
            for v in range(seg_rows // SUBLANES):
                lo = v * SUBLANES
                hv = seg_a[lo:lo + SUBLANES, :] * carry + seg_h[lo:lo + SUBLANES, :]
                ends.append(hv)
                carry = hv[SUBLANES - 1:SUBLANES, :]
            hstate_ref[:, cg] = carry
            seg_in = shift_rows(jnp.concatenate(ends, axis=0), state_in)
            h_lru = jnp.concatenate([h_loc[i] + a_cum[i] * seg_in for i in range(SEG_LEN)], axis=0)
            ylru_ref[:, cg] = (h_lru * jax.nn.gelu(proj(C_GATE + g * GROUP_W, unp_ref))).astype(bf16)

        def merge_gates():
            sgl_ref[:, cg] = jax.nn.sigmoid(proj(C_GL + g * GROUP_W))
            sga_ref[:, cg] = jax.nn.sigmoid(proj(C_GA + g * GROUP_W))

        return [conv_and_gates, decay_and_input, local_scan, carry_and_gate, merge_gates]

    first_head2 = lax.broadcasted_iota(jnp.int32, (2 * BLOCK, LANES), 1) < HEAD_DIM
    qi = lax.broadcasted_iota(jnp.int32, (2 * BLOCK, 2 * BLOCK), 0) % BLOCK
    si = lax.broadcasted_iota(jnp.int32, (2 * BLOCK, 2 * BLOCK), 1)
    diff = BLOCK + qi - si
    band = (diff >= 0) & (diff < WINDOW)
    first_pair_rows = lax.broadcasted_iota(jnp.int32, (2 * BLOCK, 1), 0) < BLOCK

    def scores(j, qb):
        r0 = qb * BLOCK
        q2 = jnp.concatenate(
            [q_ref[r0:r0 + BLOCK, (2 * j + pp) * LANES:(2 * j + pp + 1) * LANES] for pp in range(2)],
            axis=0)
        mask = band & ((si >= BLOCK) | (t_idx * (ts // BLOCK) + qb > 0))
        halves = []
        for second in range(2):
            slab = slice((2 * j + second) * LANES, (2 * j + second + 1) * LANES)
            head = 4 * j + second
            sink = jnp.where(first_pair_rows, sinks_ref[head] * LOG2E, sinks_ref[head + 2] * LOG2E)
            s = lax.dot_general(q2, kall_ref[r0:r0 + 2 * BLOCK, slab], (((1,), (1,)), ((), ())),
                                preferred_element_type=f32)
            s = jnp.where(mask, s, MASK_VALUE)
            m = jnp.maximum(jnp.max(s, axis=1, keepdims=True), sink)
            halves.append((s, m, sink))
        return halves

    def weighted_values(j, qb, halves):
        r0 = qb * BLOCK
        parts = []
        for second, (s, m, sink) in enumerate(halves):
            slab = slice((2 * j + second) * LANES, (2 * j + second + 1) * LANES)
            e = jnp.exp2(s - m).astype(bf16)
            pv = jnp.dot(e, vall_ref[r0:r0 + 2 * BLOCK, slab], preferred_element_type=f32)
            sum_lane = SUM_LANE_HIGH if second else SUM_LANE_LOW
            denom = pv[:, sum_lane:sum_lane + 1] + jnp.exp2(sink - m)
            parts.append(pv * (1.0 / denom))
        out = jnp.where(first_head2, parts[0], parts[1]).astype(bf16)
        for pp in range(2):
            lo = (2 * j + pp) * LANES
            yattn_ref[r0:r0 + BLOCK, lo:lo + LANES] = out[pp * BLOCK:(pp + 1) * BLOCK, :]

    def attention_stages():
        blocks = [(j, qb) for j in range(N_KV_HEADS) for qb in range(ts // BLOCK)]
        pending = {}

        def stage(n):
            def run():
                if n < len(blocks):
                    pending[n] = scores(*blocks[n])
                if n >= 1:
                    weighted_values(*blocks[n - 1], pending.pop(n - 1))
            return run

        return [stage(n) for n in range(len(blocks) + 1)]

    def lru_projection(c):
        p_lru = jnp.dot(ylru_ref[...], w_proj_ref[:, P_LRU + c * GROUP_W:P_LRU + (c + 1) * GROUP_W],
                        preferred_element_type=f32)
        for half in range(GROUP_W // LANES):
            lg = c * (GROUP_W // LANES) + half
            for i in range(SEG_LEN):
                for grp in range(seg_rows // SUBLANES):
                    row = i * seg_rows + grp * SUBLANES
                    plru_ref.at[lg][pl.ds(SEG_LEN * SUBLANES * grp + i, SUBLANES, stride=SEG_LEN), :] = (
                        p_lru[row:row + SUBLANES, half * LANES:(half + 1) * LANES])

    lru_list = [stage for g in range(GROUPS) for stage in lru_stages(g)]
    lru_list += [lambda c=c: lru_projection(c) for c in range(GROUPS)]
    att_list = attention_stages()
    att_per_lru = -(-len(att_list) // len(lru_list))
    while lru_list or att_list:
        if lru_list:
            lru_list.pop(0)()
        for _ in range(att_per_lru):
            if att_list:
                att_list.pop(0)()

    kall_ref[0:BLOCK, :] = kall_ref[ts:ts + BLOCK, :]
    vall_ref[0:BLOCK, :] = vall_ref[ts:ts + BLOCK, :]

    for n in range(GROUPS):
        cn = slice(n * GROUP_W, (n + 1) * GROUP_W)
        p_attn = jnp.dot(yattn_ref[...], w_proj_ref[:, P_ATTN + n * GROUP_W:P_ATTN + (n + 1) * GROUP_W],
                         preferred_element_type=f32)
        p_lru = jnp.concatenate([plru_ref[n * (GROUP_W // LANES) + half] for half in range(GROUP_W // LANES)],
                                axis=1)
        merged_ref[:, cn] = (sgl_ref[:, cn] * p_lru + sga_ref[:, cn] * p_attn).astype(bf16)
    m_out = [jnp.dot(merged_ref[...], w_proj_ref[:, P_OUT + n * GROUP_W:P_OUT + (n + 1) * GROUP_W],
                     preferred_element_type=f32) for n in range(GROUPS)]
    mean_sq = sum(jnp.sum(mo * mo, axis=1, keepdims=True) for mo in m_out) * (1.0 / D_MODEL)
    inv_rms = lax.rsqrt(mean_sq + NORM_EPS)
    for n in range(GROUPS):
        cn = slice(n * GROUP_W, (n + 1) * GROUP_W)
        o_ref[:, cn] = h_ref[:, cn] + m_out[n] * inv_rms * post_g_ref[:, cn]


def _mixer(h3d, sinks, cos_t, sin_t, pre_g, w_in_p, conv_w, conv_b, w_gate, b_a, b_x, lam,
           w_proj, post_g):
    bsz, s, _ = h3d.shape
    ts = MIX_TOKENS
    tile = pl.BlockSpec((None, ts, D_MODEL), lambda b, t: (b, t, 0))
    rope_spec = pl.BlockSpec((ts, LANES), lambda b, t: (t, 0))
    return pl.pallas_call(
        _mixer_kernel,
        name="mixer",
        grid=(bsz, s // ts),
        in_specs=[
            pl.BlockSpec(memory_space=pltpu.SMEM),
            tile, rope_spec, rope_spec,
            _const_spec((1, D_MODEL)),
            _const_spec((D_MODEL, IN_WIDTH)),
            _const_spec((CONV_WIDTH, LRU_WIDTH)),
            _const_spec((1, LRU_WIDTH)),
            _const_spec((LRU_WIDTH // MXU_DIM, MXU_DIM, 2 * MXU_DIM)),
            _const_spec((1, LRU_WIDTH)),
            _const_spec((1, LRU_WIDTH)),
            _const_spec((1, LRU_WIDTH)),
            _const_spec((D_MODEL, 3 * D_MODEL + PAD_COLS)),
            _const_spec((1, D_MODEL)),
        ],
        out_specs=tile,
        out_shape=jax.ShapeDtypeStruct(h3d.shape, jnp.float32),
        scratch_shapes=[
            pltpu.VMEM((ts, D_MODEL), jnp.bfloat16),
            pltpu.VMEM((D_MODEL // LANES, ts, LANES), jnp.float32),
            pltpu.VMEM((ts, D_MODEL), jnp.bfloat16),
            pltpu.VMEM((CONV_WIDTH - 1, 1, LRU_WIDTH), jnp.float32),
            pltpu.VMEM((1, LRU_WIDTH), jnp.float32),
            pltpu.VMEM((ts, ATTN_WIDTH), jnp.bfloat16),
            pltpu.VMEM((BLOCK + ts, KV_SLAB_WIDTH), jnp.bfloat16),
            pltpu.VMEM((BLOCK + ts, KV_SLAB_WIDTH), jnp.bfloat16),
            pltpu.VMEM((ts, LRU_WIDTH), jnp.bfloat16),
            pltpu.VMEM((ts, ATTN_WIDTH), jnp.bfloat16),
            pltpu.VMEM((ts, D_MODEL), jnp.float32),
            pltpu.VMEM((ts, D_MODEL), jnp.float32),
            pltpu.VMEM((D_MODEL // LANES, ts, LANES), jnp.float32),
            pltpu.VMEM((ts, D_MODEL), jnp.bfloat16),
        ],
        compiler_params=pltpu.CompilerParams(
            dimension_semantics=("arbitrary", "arbitrary"), vmem_limit_bytes=VMEM_LIMIT_BYTES),
    )(sinks, h3d, cos_t, sin_t, pre_g, w_in_p, conv_w, conv_b, w_gate, b_a, b_x, lam,
      w_proj, post_g)


def _pad_cast(*ws):
    zeros = jnp.zeros((ws[0].shape[0], PAD_COLS), jnp.bfloat16)
    return jnp.concatenate([w.astype(jnp.bfloat16) for w in ws] + [zeros], axis=1)


def _gate_weights(w_a, w_x):
    per = MXU_DIM // LRU_BLOCK_W
    eye = jnp.eye(per, dtype=w_a.dtype)

    def expand(w):
        w4 = w.reshape(LRU_BLOCKS // per, per, LRU_BLOCK_W, LRU_BLOCK_W)
        return jnp.einsum("gncd,nm->gncmd", w4, eye).reshape(LRU_BLOCKS // per, MXU_DIM, MXU_DIM)

    return jnp.concatenate([expand(w_a), expand(w_x)], axis=2).astype(jnp.bfloat16)


def _rope_tables(seq_len):
    inv_freq = ROPE_THETA ** (-jnp.arange(HALF, dtype=jnp.float32) / HALF)
    ang = jnp.arange(seq_len, dtype=jnp.float32)[:, None] * inv_freq[None, :]
    cos, sin = jnp.cos(ang), jnp.sin(ang)
    return (jnp.concatenate([cos, cos, cos, cos], axis=1),
            jnp.concatenate([-sin, sin, -sin, sin], axis=1))


def kernel(x, ffn1_pre_g, ffn1_w_gu, ffn1_w_down, ffn1_post_g, mix_pre_g, w_in, conv_w, conv_b, lru_w_a, lru_b_a, lru_w_x, lru_b_x, lru_lambda, attn_sinks, w_proj_lru, w_proj_attn, w_out, mix_post_g, ffn2_pre_g, ffn2_w_gu, ffn2_w_down, ffn2_post_g):
    bsz, s, d = x.shape
    depth = ffn1_pre_g.shape[0]
    cos_t, sin_t = _rope_tables(s)
    bf = jnp.bfloat16
    h = x
    for l in range(depth):
        h = _ffn(h.reshape(bsz * s, d), ffn1_pre_g[l][None], ffn1_w_gu[l].astype(bf),
                 _pad_cast(ffn1_w_down[l]), ffn1_post_g[l][None]).reshape(bsz, s, d)
        h = _mixer(h, attn_sinks[l], cos_t, sin_t, mix_pre_g[l][None], w_in[l].astype(bf),
                   conv_w[l], conv_b[l][None], _gate_weights(lru_w_a[l], lru_w_x[l]),
                   lru_b_a[l][None], lru_b_x[l][None], lru_lambda[l][None],
                   _pad_cast(w_proj_lru[l], w_proj_attn[l], w_out[l]),
                   mix_post_g[l][None])
        h = _ffn(h.reshape(bsz * s, d), ffn2_pre_g[l][None], ffn2_w_gu[l].astype(bf),
                 _pad_cast(ffn2_w_down[l]), ffn2_post_g[l][None]).reshape(bsz, s, d)
    return h
```

```python
import jax
import jax.numpy as jnp
from jax import lax
from jax.experimental import pallas as pl
from jax.experimental.pallas import tpu as pltpu

D_MODEL = 1024
LRU_WIDTH = 1024
LRU_BLOCKS = 16
LRU_BLOCK_W = LRU_WIDTH // LRU_BLOCKS
CONV_WIDTH = 4
LRU_C = 8.0
HEAD_DIM = 64
HALF = HEAD_DIM // 2
N_Q_HEADS = 16
N_KV_HEADS = 4
ATTN_WIDTH = N_Q_HEADS * HEAD_DIM
KV_WIDTH = N_KV_HEADS * HEAD_DIM
WINDOW = 128
BLOCK = 128
ROPE_THETA = 10000.0
D_FF = 2816
MACARON_SCALE = 0.5
NORM_EPS = 1e-6
MASK_VALUE = -1e30

LANES = 128
SUBLANES = 8
MXU_DIM = 256
VMEM_LIMIT_BYTES = 56 * 1024 * 1024

FFN_TOKENS = 512
FFN_CHUNK = 256
FFN_PIECE = 64
MIX_TOKENS = 512
SEG_LEN = SUBLANES

GROUP_W = 2 * MXU_DIM
GROUPS = LRU_WIDTH // GROUP_W
C_GATE = 0
C_X = C_GATE + LRU_WIDTH
C_Q = C_X + LRU_WIDTH
C_K = C_Q + ATTN_WIDTH
C_V = C_K + KV_WIDTH
C_GL = C_V + KV_WIDTH
C_GA = C_GL + D_MODEL
IN_WIDTH = C_GA + D_MODEL
LOG2E = 1.4426950408889634
P_LRU, P_ATTN, P_OUT = 0, D_MODEL, 2 * D_MODEL
KV_SLAB_WIDTH = N_KV_HEADS * 2 * LANES
SUM_LANE_LOW, SUM_LANE_HIGH = HEAD_DIM, 0
PAD_COLS = LANES


def _rms_norm(x, g):
    return x * lax.rsqrt(jnp.mean(x * x, axis=-1, keepdims=True) + NORM_EPS) * g


def _sigmoid(x):
    return 0.5 * jnp.tanh(0.5 * x) + 0.5


def _dependent_zero(values, dtype):
    bits = None
    for v in values:
        b = pltpu.bitcast(v[0:SUBLANES, 0:LANES], jnp.uint32)
        bits = b if bits is None else bits | b
    half = jnp.uint32(16)
    zero = pltpu.bitcast(lax.shift_right_logical(lax.shift_right_logical(bits, half), half), jnp.float32)
    if dtype == jnp.float32:
        return zero
    return jnp.concatenate([zero, zero], axis=0).astype(dtype)


def _const_spec(shape):
    n = len(shape)
    return pl.BlockSpec(shape, lambda *_: (0,) * n, pipeline_mode=pl.Buffered(1))


def _ffn_kernel(h_ref, h_next_ref, h_prev_ref, pre_g_ref, w_gu_ref, w_down_ref, post_g_ref, o_ref,
                xn_ref, xn_next_ref, act_ref, f_ref):
    i = pl.program_id(0)
    n_tiles = pl.num_programs(0) - 1
    bf16, f32 = jnp.bfloat16, jnp.float32
    n_pieces = FFN_TOKENS // FFN_PIECE
    n_chunks = D_FF // FFN_CHUNK

    def finish_rows(rows):
        out = h_prev_ref[rows, :] + MACARON_SCALE * _rms_norm(f_ref[rows, :], post_g_ref[...])
        o_ref[rows, :] = out
        return out

    @pl.when(i == 0)
    def _():
        xn_ref[...] = _rms_norm(h_ref[...], pre_g_ref[...]).astype(bf16)
        f_ref[...] = jnp.zeros_like(f_ref)

    @pl.when(i < n_tiles)
    def _():
        for c in range(n_chunks):
            lo = c * FFN_CHUNK
            xn = xn_ref[...]
            g = jnp.dot(xn, w_gu_ref[:, lo:lo + FFN_CHUNK], preferred_element_type=f32)
            u = jnp.dot(xn, w_gu_ref[:, D_FF + lo:D_FF + lo + FFN_CHUNK], preferred_element_type=f32)
            act_ref[:, lo:lo + FFN_CHUNK] = (jax.nn.silu(g) * u).astype(bf16)
            anchors = []
            if c < n_pieces:
                anchors.append(finish_rows(slice(c * FFN_PIECE, (c + 1) * FFN_PIECE)))
            p = c - (n_chunks - n_pieces)
            if p >= 0:
                rows = slice(p * FFN_PIECE, (p + 1) * FFN_PIECE)
                xn_piece = _rms_norm(h_next_ref[rows, :], pre_g_ref[...])
                xn_next_ref[rows, :] = xn_piece.astype(bf16)
                anchors.append(xn_piece)
            if c + 1 < n_chunks:
                tile = (slice(0, 2 * SUBLANES), slice(0, LANES))
                xn_ref[tile] = xn_ref[tile] + _dependent_zero(anchors, bf16)
        for n in range(D_MODEL // GROUP_W):
            cn = slice(n * GROUP_W, (n + 1) * GROUP_W)
            f_ref[:, cn] = jnp.dot(act_ref[...], w_down_ref[:, cn], preferred_element_type=f32)
        xn_ref[...] = xn_next_ref[...]

    @pl.when(i == n_tiles)
    def _():
        for p in range(n_pieces):
            finish_rows(slice(p * FFN_PIECE, (p + 1) * FFN_PIECE))


def _ffn(h2d, pre_g, w_gu, w_down, post_g):
    m = h2d.shape[0]
    n_tiles = m // FFN_TOKENS
    tile = (FFN_TOKENS, D_MODEL)
    return pl.pallas_call(
        _ffn_kernel,
        name="ffn",
        grid=(n_tiles + 1,),
        in_specs=[
            pl.BlockSpec(tile, lambda i: (jnp.minimum(i, n_tiles - 1), 0)),
            pl.BlockSpec(tile, lambda i: (jnp.minimum(i + 1, n_tiles - 1), 0)),
            pl.BlockSpec(tile, lambda i: (jnp.maximum(i - 1, 0), 0)),
            _const_spec((1, D_MODEL)),
            _const_spec((D_MODEL, 2 * D_FF)),
            _const_spec((D_FF, D_MODEL + PAD_COLS)),
            _const_spec((1, D_MODEL)),
        ],
        out_specs=pl.BlockSpec(tile, lambda i: (jnp.maximum(i - 1, 0), 0)),
        out_shape=jax.ShapeDtypeStruct((m, D_MODEL), jnp.float32),
        scratch_shapes=[
            pltpu.VMEM(tile, jnp.bfloat16),
            pltpu.VMEM(tile, jnp.bfloat16),
            pltpu.VMEM((FFN_TOKENS, D_FF), jnp.bfloat16),
            pltpu.VMEM(tile, jnp.float32),
        ],
        compiler_params=pltpu.CompilerParams(
            dimension_semantics=("arbitrary",), vmem_limit_bytes=VMEM_LIMIT_BYTES),
    )(h2d, h2d, h2d, pre_g, w_gu, w_down, post_g)


def _mixer_kernel(sinks_ref, h_ref, cos_ref, sin_ref, pre_g_ref, w_in_ref, conv_w_ref,
                  conv_b_ref, w_gate_ref, b_a_ref, b_x_ref, lam_ref, w_proj_ref, post_g_ref, o_ref,
                  un_ref, unf_ref, unp_ref, xhist_ref, hstate_ref, q_ref, kall_ref, vall_ref, ylru_ref,
                  yattn_ref, sgl_ref, sga_ref, plru_ref, merged_ref):
    ts = MIX_TOKENS
    seg_rows = ts // SEG_LEN
    t_idx = pl.program_id(1)
    f32, bf16 = jnp.float32, jnp.bfloat16

    @pl.when(t_idx == 0)
    def _():
        xhist_ref[...] = jnp.zeros_like(xhist_ref)
        hstate_ref[...] = jnp.zeros_like(hstate_ref)
        kall_ref[0:BLOCK, :] = jnp.zeros((BLOCK, KV_SLAB_WIDTH), bf16)
        vall_ref[0:BLOCK, :] = jnp.zeros((BLOCK, KV_SLAB_WIDTH), bf16)

    un = _rms_norm(h_ref[...], pre_g_ref[...])
    un_ref[...] = un.astype(bf16)
    for lg in range(D_MODEL // LANES):
        unf_ref[lg] = un[:, lg * LANES:(lg + 1) * LANES]
    for lg in range(D_MODEL // LANES):
        pieces = [unf_ref.at[lg][pl.ds(SEG_LEN * SUBLANES * grp + i, SUBLANES, stride=SEG_LEN), :]
                  for i in range(SEG_LEN) for grp in range(seg_rows // SUBLANES)]
        unp_ref[:, lg * LANES:(lg + 1) * LANES] = jnp.concatenate(pieces, axis=0).astype(bf16)

    def proj(lo, src_ref=un_ref):
        return jnp.dot(src_ref[...], w_in_ref[:, lo:lo + GROUP_W], preferred_element_type=f32)

    lane = lax.broadcasted_iota(jnp.int32, (ts, LANES), 1)
    first_half = (lane % HEAD_DIM) < HALF
    first_head = lane < HEAD_DIM

    def rope(x, c, s):
        partner = jnp.where(first_half, pltpu.roll(x, LANES - HALF, axis=1),
                            pltpu.roll(x, HALF, axis=1))
        return x * c + partner * s

    cos = cos_ref[...]
    sin = sin_ref[...]
    q_scale = (HEAD_DIM ** -0.5) * LOG2E
    cos_q = cos * q_scale
    sin_q = sin * q_scale
    seg_row = lax.broadcasted_iota(jnp.int32, (seg_rows, GROUP_W), 0)

    def shift_rows(cur, first_row):
        return jnp.where(seg_row == 0, first_row, pltpu.roll(cur, 1, axis=0))

    kv_nat = proj(C_K)
    k_nat, v_nat = kv_nat[:, :KV_WIDTH], kv_nat[:, KV_WIDTH:]
    for jj in range(N_KV_HEADS // 2):
        kr = rope(k_nat[:, jj * LANES:(jj + 1) * LANES], cos, sin)
        vv = v_nat[:, jj * LANES:(jj + 1) * LANES]
        kr_swapped = pltpu.roll(kr, HEAD_DIM, axis=1)
        vv_swapped = pltpu.roll(vv, HEAD_DIM, axis=1)
        for half in range(2):
            j = 2 * jj + half
            for dst_ref, same, swapped, fill_low, fill_high in (
                    (kall_ref, kr, kr_swapped, 0.0, 0.0),
                    (vall_ref, vv, vv_swapped, (lane == SUM_LANE_LOW).astype(f32),
                     (lane == SUM_LANE_HIGH).astype(f32))):
                low, high = (same, swapped) if half == 0 else (swapped, same)
                dst_ref[BLOCK:BLOCK + ts, (2 * j) * LANES:(2 * j + 1) * LANES] = (
                    jnp.where(first_head, low, fill_low).astype(bf16))
                dst_ref[BLOCK:BLOCK + ts, (2 * j + 1) * LANES:(2 * j + 2) * LANES] = (
                    jnp.where(first_head, fill_high, high).astype(bf16))

    for g in range(GROUPS):
        qg = proj(C_Q + g * GROUP_W)
        for pp in range(GROUP_W // LANES):
            lo = g * GROUP_W + pp * LANES
            q_ref[:, lo:lo + LANES] = rope(qg[:, pp * LANES:(pp + 1) * LANES], cos_q, sin_q).astype(bf16)

    def lru_stages(g):
        cg = slice(g * GROUP_W, (g + 1) * GROUP_W)
        st = {}

        def conv_and_gates():
            xg = proj(C_X + g * GROUP_W, unp_ref)
            x_blk = [xg[i * seg_rows:(i + 1) * seg_rows, :] for i in range(SEG_LEN)]
            wrapped = {}
            for i_src in range(SEG_LEN - CONV_WIDTH + 1, SEG_LEN):
                k = i_src - (SEG_LEN - CONV_WIDTH + 1)
                wrapped[i_src] = shift_rows(x_blk[i_src], xhist_ref[k, :, cg])
                xhist_ref[k, :, cg] = x_blk[i_src][seg_rows - 1:seg_rows, :]
            xc_blk = []
            for i in range(SEG_LEN):
                acc = conv_b_ref[:, cg]
                for d in range(CONV_WIDTH):
                    src = x_blk[i - d] if i >= d else wrapped[i - d + SEG_LEN]
                    acc = acc + src * conv_w_ref[CONV_WIDTH - 1 - d:CONV_WIDTH - d, cg]
                xc_blk.append(acc)
            st["xc"] = jnp.concatenate(xc_blk, axis=0)
            xc_b = st["xc"].astype(bf16)
            st["gates"] = [
                jnp.dot(xc_b[:, k * MXU_DIM:(k + 1) * MXU_DIM], w_gate_ref[g * (GROUP_W // MXU_DIM) + k],
                        preferred_element_type=f32) for k in range(GROUP_W // MXU_DIM)]

        def decay_and_input():
            gates, xc = st["gates"], st["xc"]
            th_r = jnp.tanh(jnp.concatenate([ri[:, :MXU_DIM] for ri in gates], axis=1)
                            + 0.5 * b_a_ref[:, cg])
            th_i = jnp.tanh(jnp.concatenate([ri[:, MXU_DIM:] for ri in gates], axis=1)
                            + 0.5 * b_x_ref[:, cg])
            lam = lam_ref[:, cg]
            softplus_neg_lam = jnp.maximum(-lam, 0.0) + jnp.log1p(jnp.exp(-jnp.abs(lam)))
            k = (0.5 * LRU_C) * softplus_neg_lam
            neg_log_a = th_r * k + k
            a = jnp.exp2(neg_log_a * (-LOG2E))
            t = jnp.tanh(neg_log_a) * (a * a + 1.0)
            mult = jnp.where(t > 0.0, t * lax.rsqrt(t), 0.0)
            half_xc = 0.5 * xc
            st["a"], st["b"] = a, mult * (th_i * half_xc + half_xc)

        def local_scan():
            a, b = st["a"], st["b"]
            a_blk = [a[i * seg_rows:(i + 1) * seg_rows, :] for i in range(SEG_LEN)]
            b_blk = [b[i * seg_rows:(i + 1) * seg_rows, :] for i in range(SEG_LEN)]
            h_loc, a_cum = [b_blk[0]], [a_blk[0]]
            for i in range(1, SEG_LEN):
                h_loc.append(a_blk[i] * h_loc[-1] + b_blk[i])
                a_cum.append(a_blk[i] * a_cum[-1])
            st["h_loc"], st["a_cum"] = h_loc, a_cum

        def carry_and_gate():
            h_loc, a_cum = st["h_loc"], st["a_cum"]
            seg_a, seg_h = a_cum[-1], h_loc[-1]
            shift = 1
            while shift < SUBLANES:
                keep = (seg_row % SUBLANES) >= shift
                a_prev = jnp.where(keep, pltpu.roll(seg_a, shift, axis=0), 1.0)
                h_prev = jnp.where(keep, pltpu.roll(seg_h, shift, axis=0), 0.0)
                seg_h = seg_a * h_prev + seg_h
                seg_a = seg_a * a_prev
                shift *= 2
            carry = hstate_ref[:, cg]
            state_in = carry
            ends = []
            for v in range(seg_rows // SUBLANES):
                lo = v * SUBLANES
                hv = seg_a[lo:lo + SUBLANES, :] * carry + seg_h[lo:lo + SUBLANES, :]
                ends.append(hv)
                carry = hv[SUBLANES - 1:SUBLANES, :]
            hstate_ref[:, cg] = carry
            seg_in = shift_rows(jnp.concatenate(ends, axis=0), state_in)
            h_lru = jnp.concatenate([h_loc[i] + a_cum[i] * seg_in for i in range(SEG_LEN)], axis=0)
            ylru_ref[:, cg] = (h_lru * jax.nn.gelu(proj(C_GATE + g * GROUP_W, unp_ref))).astype(bf16)

        def merge_gates():
            sgl_ref[:, cg] = _sigmoid(proj(C_GL + g * GROUP_W))
            sga_ref[:, cg] = _sigmoid(proj(C_GA + g * GROUP_W))

        return [conv_and_gates, decay_and_input, local_scan, carry_and_gate, merge_gates]

    first_head2 = lax.broadcasted_iota(jnp.int32, (2 * BLOCK, LANES), 1) < HEAD_DIM
    qi = lax.broadcasted_iota(jnp.int32, (2 * BLOCK, 2 * BLOCK), 0) % BLOCK
    si = lax.broadcasted_iota(jnp.int32, (2 * BLOCK, 2 * BLOCK), 1)
    diff = BLOCK + qi - si
    band = (diff >= 0) & (diff < WINDOW)
    first_pair_rows = lax.broadcasted_iota(jnp.int32, (2 * BLOCK, 1), 0) < BLOCK

    def scores(j, qb):
        r0 = qb * BLOCK
        q2 = jnp.concatenate(
            [q_ref[r0:r0 + BLOCK, (2 * j + pp) * LANES:(2 * j + pp + 1) * LANES] for pp in range(2)],
            axis=0)
        mask = band & ((si >= BLOCK) | (t_idx * (ts // BLOCK) + qb > 0))
        halves = []
        for second in range(2):
            slab = slice((2 * j + second) * LANES, (2 * j + second + 1) * LANES)
            head = 4 * j + second
            sink = jnp.where(first_pair_rows, sinks_ref[head] * LOG2E, sinks_ref[head + 2] * LOG2E)
            s = lax.dot_general(q2, kall_ref[r0:r0 + 2 * BLOCK, slab], (((1,), (1,)), ((), ())),
                                preferred_element_type=f32)
            s = jnp.where(mask, s, MASK_VALUE)
            m = jnp.maximum(jnp.max(s, axis=1, keepdims=True), sink)
            halves.append((s, m, sink))
        return halves

    def weighted_values(j, qb, halves):
        r0 = qb * BLOCK
        parts = []
        for second, (s, m, sink) in enumerate(halves):
            slab = slice((2 * j + second) * LANES, (2 * j + second + 1) * LANES)
            e = jnp.exp2(s - m).astype(bf16)
            pv = jnp.dot(e, vall_ref[r0:r0 + 2 * BLOCK, slab], preferred_element_type=f32)
            sum_lane = SUM_LANE_HIGH if second else SUM_LANE_LOW
            denom = pv[:, sum_lane:sum_lane + 1] + jnp.exp2(sink - m)
            parts.append(pv * (1.0 / denom))
        out = jnp.where(first_head2, parts[0], parts[1]).astype(bf16)
        for pp in range(2):
            lo = (2 * j + pp) * LANES
            yattn_ref[r0:r0 + BLOCK, lo:lo + LANES] = out[pp * BLOCK:(pp + 1) * BLOCK, :]

    def attention_stages():
        blocks = [(j, qb) for j in range(N_KV_HEADS) for qb in range(ts // BLOCK)]
        pending = {}

        def stage(n):
            def run():
                if n < len(blocks):
                    pending[n] = scores(*blocks[n])
                if n >= 1:
                    weighted_values(*blocks[n - 1], pending.pop(n - 1))
            return run

        return [stage(n) for n in range(len(blocks) + 1)]

    def lru_projection(c):
        p_lru = jnp.dot(ylru_ref[...], w_proj_ref[:, P_LRU + c * GROUP_W:P_LRU + (c + 1) * GROUP_W],
                        preferred_element_type=f32)
        for half in range(GROUP_W // LANES):
            lg = c * (GROUP_W // LANES) + half
            for i in range(SEG_LEN):
                for grp in range(seg_rows // SUBLANES):
                    row = i * seg_rows + grp * SUBLANES
                    plru_ref.at[lg][pl.ds(SEG_LEN * SUBLANES * grp + i, SUBLANES, stride=SEG_LEN), :] = (
                        p_lru[row:row + SUBLANES, half * LANES:(half + 1) * LANES])

    lru_list = [stage for g in range(GROUPS) for stage in lru_stages(g)]
    lru_list += [lambda c=c: lru_projection(c) for c in range(GROUPS)]
    att_list = attention_stages()
    att_per_lru = -(-len(att_list) // len(lru_list))
    while lru_list or att_list:
        if lru_list:
            lru_list.pop(0)()
        for _ in range(att_per_lru):
            if att_list:
                att_list.pop(0)()

    kall_ref[0:BLOCK, :] = kall_ref[ts:ts + BLOCK, :]
    vall_ref[0:BLOCK, :] = vall_ref[ts:ts + BLOCK, :]

    for n in range(GROUPS):
        cn = slice(n * GROUP_W, (n + 1) * GROUP_W)
        p_attn = jnp.dot(yattn_ref[...], w_proj_ref[:, P_ATTN + n * GROUP_W:P_ATTN + (n + 1) * GROUP_W],
                         preferred_element_type=f32)
        p_lru = jnp.concatenate([plru_ref[n * (GROUP_W // LANES) + half] for half in range(GROUP_W // LANES)],
                                axis=1)
        merged_ref[:, cn] = (sgl_ref[:, cn] * p_lru + sga_ref[:, cn] * p_attn).astype(bf16)
    m_out = [jnp.dot(merged_ref[...], w_proj_ref[:, P_OUT + n * GROUP_W:P_OUT + (n + 1) * GROUP_W],
                     preferred_element_type=f32) for n in range(GROUPS)]
    mean_sq = sum(jnp.sum(mo * mo, axis=1, keepdims=True) for mo in m_out) * (1.0 / D_MODEL)
    inv_rms = lax.rsqrt(mean_sq + NORM_EPS)
    for n in range(GROUPS):
        cn = slice(n * GROUP_W, (n + 1) * GROUP_W)
        o_ref[:, cn] = h_ref[:, cn] + m_out[n] * inv_rms * post_g_ref[:, cn]


def _mixer(h3d, sinks, cos_t, sin_t, pre_g, w_in_p, conv_w, conv_b, w_gate, b_a, b_x, lam,
           w_proj, post_g):
    bsz, s, _ = h3d.shape
    ts = MIX_TOKENS
    tile = pl.BlockSpec((None, ts, D_MODEL), lambda b, t: (b, t, 0))
    rope_spec = pl.BlockSpec((ts, LANES), lambda b, t: (t, 0))
    return pl.pallas_call(
        _mixer_kernel,
        name="mixer",
        grid=(bsz, s // ts),
        in_specs=[
            pl.BlockSpec(memory_space=pltpu.SMEM),
            tile, rope_spec, rope_spec,
            _const_spec((1, D_MODEL)),
            _const_spec((D_MODEL, IN_WIDTH)),
            _const_spec((CONV_WIDTH, LRU_WIDTH)),
            _const_spec((1, LRU_WIDTH)),
            _const_spec((LRU_WIDTH // MXU_DIM, MXU_DIM, 2 * MXU_DIM)),
            _const_spec((1, LRU_WIDTH)),
            _const_spec((1, LRU_WIDTH)),
            _const_spec((1, LRU_WIDTH)),
            _const_spec((D_MODEL, 3 * D_MODEL + PAD_COLS)),
            _const_spec((1, D_MODEL)),
        ],
        out_specs=tile,
        out_shape=jax.ShapeDtypeStruct(h3d.shape, jnp.float32),
        scratch_shapes=[
            pltpu.VMEM((ts, D_MODEL), jnp.bfloat16),
            pltpu.VMEM((D_MODEL // LANES, ts, LANES), jnp.float32),
            pltpu.VMEM((ts, D_MODEL), jnp.bfloat16),
            pltpu.VMEM((CONV_WIDTH - 1, 1, LRU_WIDTH), jnp.float32),
            pltpu.VMEM((1, LRU_WIDTH), jnp.float32),
            pltpu.VMEM((ts, ATTN_WIDTH), jnp.bfloat16),
            pltpu.VMEM((BLOCK + ts, KV_SLAB_WIDTH), jnp.bfloat16),
            pltpu.VMEM((BLOCK + ts, KV_SLAB_WIDTH), jnp.bfloat16),
            pltpu.VMEM((ts, LRU_WIDTH), jnp.bfloat16),
            pltpu.VMEM((ts, ATTN_WIDTH), jnp.bfloat16),
            pltpu.VMEM((ts, D_MODEL), jnp.float32),
            pltpu.VMEM((ts, D_MODEL), jnp.float32),
            pltpu.VMEM((D_MODEL // LANES, ts, LANES), jnp.float32),
            pltpu.VMEM((ts, D_MODEL), jnp.bfloat16),
        ],
        compiler_params=pltpu.CompilerParams(
            dimension_semantics=("arbitrary", "arbitrary"), vmem_limit_bytes=VMEM_LIMIT_BYTES),
    )(sinks, h3d, cos_t, sin_t, pre_g, w_in_p, conv_w, conv_b, w_gate, b_a, b_x, lam,
      w_proj, post_g)


def _pad_cast(*ws):
    zeros = jnp.zeros((ws[0].shape[0], PAD_COLS), jnp.bfloat16)
    return jnp.concatenate([w.astype(jnp.bfloat16) for w in ws] + [zeros], axis=1)


def _gate_weights(w_a, w_x):
    per = MXU_DIM // LRU_BLOCK_W
    eye = jnp.eye(per, dtype=w_a.dtype)

    def expand(w):
        w4 = w.reshape(LRU_BLOCKS // per, per, LRU_BLOCK_W, LRU_BLOCK_W)
        return jnp.einsum("gncd,nm->gncmd", w4, eye).reshape(LRU_BLOCKS // per, MXU_DIM, MXU_DIM)

    return (0.5 * jnp.concatenate([expand(w_a), expand(w_x)], axis=2)).astype(jnp.bfloat16)


def _rope_tables(seq_len):
    inv_freq = ROPE_THETA ** (-jnp.arange(HALF, dtype=jnp.float32) / HALF)
    ang = jnp.arange(seq_len, dtype=jnp.float32)[:, None] * inv_freq[None, :]
    cos, sin = jnp.cos(ang), jnp.sin(ang)
    return (jnp.concatenate([cos, cos, cos, cos], axis=1),
            jnp.concatenate([-sin, sin, -sin, sin], axis=1))


def kernel(x, ffn1_pre_g, ffn1_w_gu, ffn1_w_down, ffn1_post_g, mix_pre_g, w_in, conv_w, conv_b, lru_w_a, lru_b_a, lru_w_x, lru_b_x, lru_lambda, attn_sinks, w_proj_lru, w_proj_attn, w_out, mix_post_g, ffn2_pre_g, ffn2_w_gu, ffn2_w_down, ffn2_post_g):
    bsz, s, d = x.shape
    depth = ffn1_pre_g.shape[0]
    cos_t, sin_t = _rope_tables(s)
    bf = jnp.bfloat16
    h = x
    for l in range(depth):
        h = _ffn(h.reshape(bsz * s, d), ffn1_pre_g[l][None], ffn1_w_gu[l].astype(bf),
                 _pad_cast(ffn1_w_down[l]), ffn1_post_g[l][None]).reshape(bsz, s, d)
        h = _mixer(h, attn_sinks[l], cos_t, sin_t, mix_pre_g[l][None], w_in[l].astype(bf),
                   conv_w[l], conv_b[l][None], _gate_weights(lru_w_a[l], lru_w_x[l]),
                   lru_b_a[l][None], lru_b_x[l][None], lru_lambda[l][None],
                   _pad_cast(w_proj_lru[l], w_proj_attn[l], w_out[l]),
                   mix_post_g[l][None])
        h = _ffn(h.reshape(bsz * s, d), ffn2_pre_g[l][None], ffn2_w_gu[l].astype(bf),
                 _pad_cast(ffn2_w_down[l]), ffn2_post_g[l][None]).reshape(bsz, s, d)
    return h
```

```python
import jax
import jax.numpy as jnp
import numpy as np
from jax import lax
from jax.experimental import pallas as pl
from jax.experimental.pallas import tpu as pltpu

D_MODEL = 1024
LRU_WIDTH = 1024
LRU_BLOCKS = 16
LRU_BLOCK_W = LRU_WIDTH // LRU_BLOCKS
CONV_WIDTH = 4
LRU_C = 8.0
HEAD_DIM = 64
HALF = HEAD_DIM // 2
N_Q_HEADS = 16
N_KV_HEADS = 4
ATTN_WIDTH = N_Q_HEADS * HEAD_DIM
KV_WIDTH = N_KV_HEADS * HEAD_DIM
WINDOW = 128
BLOCK = 128
ROPE_THETA = 10000.0
D_FF = 2816
MACARON_SCALE = 0.5
NORM_EPS = 1e-6
MASK_VALUE = -1e30

LANES = 128
SUBLANES = 8
MXU_DIM = 256
VMEM_LIMIT_BYTES = 56 * 1024 * 1024

FFN_TOKENS = 512
FFN_CHUNK = 256
GU_STAGE_ROWS = 64
DOWN_STAGE_ROWS = 176
MIX_TOKENS = 512
SEG_LEN = SUBLANES

GROUP_W = 2 * MXU_DIM
GROUPS = LRU_WIDTH // GROUP_W
C_GATE = 0
C_X = C_GATE + LRU_WIDTH
C_Q = C_X + LRU_WIDTH
C_K = C_Q + ATTN_WIDTH
C_V = C_K + KV_WIDTH
C_GL = C_V + KV_WIDTH
C_GA = C_GL + D_MODEL
IN_WIDTH = C_GA + D_MODEL
LOG2E = 1.4426950408889634
P_LRU, P_ATTN, P_OUT = 0, D_MODEL, 2 * D_MODEL
KV_SLAB_WIDTH = N_KV_HEADS * 2 * LANES
SUM_LANE_LOW, SUM_LANE_HIGH = HEAD_DIM, 0
PAD_COLS = LANES


def _rms_norm(x, g):
    return x * lax.rsqrt(jnp.mean(x * x, axis=-1, keepdims=True) + NORM_EPS) * g


def _sigmoid(x):
    return 0.5 * jnp.tanh(0.5 * x) + 0.5


def _const_spec(shape):
    n = len(shape)
    return pl.BlockSpec(shape, lambda *_: (0,) * n, pipeline_mode=pl.Buffered(1))


def _stage_weight(w_hbm, w_ref, stage_ref, sem_ref, chunk_rows):
    n_rows, n_cols = w_hbm.shape
    n_chunks = n_rows // chunk_rows

    def copy(k):
        return pltpu.make_async_copy(w_hbm.at[pl.ds(k * chunk_rows, chunk_rows), :],
                                     stage_ref.at[k % 2], sem_ref.at[k % 2])

    copy(0).start()
    for k in range(n_chunks):
        if k + 1 < n_chunks:
            copy(k + 1).start()
        copy(k).wait()
        w_ref[k * chunk_rows:(k + 1) * chunk_rows, 0:n_cols] = stage_ref[k % 2].astype(jnp.bfloat16)


def _ffn_kernel(h_ref, pre_g_ref, w_gu_hbm, w_down_hbm, post_g_ref, o_ref,
                w_gu_ref, w_down_ref, stage_gu_ref, stage_down_ref, sem_gu_ref, sem_down_ref, act_ref):
    bf16, f32 = jnp.bfloat16, jnp.float32

    @pl.when(pl.program_id(0) == 0)
    def _():
        _stage_weight(w_gu_hbm, w_gu_ref, stage_gu_ref, sem_gu_ref, GU_STAGE_ROWS)
        _stage_weight(w_down_hbm, w_down_ref, stage_down_ref, sem_down_ref, DOWN_STAGE_ROWS)

    xn = _rms_norm(h_ref[...], pre_g_ref[...]).astype(bf16)
    for c in range(D_FF // FFN_CHUNK):
        lo = c * FFN_CHUNK
        g = jnp.dot(xn, w_gu_ref[:, lo:lo + FFN_CHUNK], preferred_element_type=f32)
        u = jnp.dot(xn, w_gu_ref[:, D_FF + lo:D_FF + lo + FFN_CHUNK], preferred_element_type=f32)
        act_ref[:, lo:lo + FFN_CHUNK] = (jax.nn.silu(g) * u).astype(bf16)
    f = [jnp.dot(act_ref[...], w_down_ref[:, n * GROUP_W:(n + 1) * GROUP_W],
                 preferred_element_type=f32) for n in range(D_MODEL // GROUP_W)]
    mean_sq = sum(jnp.sum(fn * fn, axis=1, keepdims=True) for fn in f) * (1.0 / D_MODEL)
    scale = MACARON_SCALE * lax.rsqrt(mean_sq + NORM_EPS)
    for n in range(D_MODEL // GROUP_W):
        cn = slice(n * GROUP_W, (n + 1) * GROUP_W)
        o_ref[:, cn] = h_ref[:, cn] + f[n] * scale * post_g_ref[:, cn]


def _ffn(h2d, pre_g, w_gu, w_down, post_g):
    m = h2d.shape[0]
    tile = pl.BlockSpec((FFN_TOKENS, D_MODEL), lambda i: (i, 0))
    return pl.pallas_call(
        _ffn_kernel,
        name="ffn",
        grid=(m // FFN_TOKENS,),
        in_specs=[
            tile,
            _const_spec((1, D_MODEL)),
            pl.BlockSpec(memory_space=pl.ANY),
            pl.BlockSpec(memory_space=pl.ANY),
            _const_spec((1, D_MODEL)),
        ],
        out_specs=tile,
        out_shape=jax.ShapeDtypeStruct((m, D_MODEL), jnp.float32),
        scratch_shapes=[
            pltpu.VMEM((D_MODEL, 2 * D_FF), jnp.bfloat16),
            pltpu.VMEM((D_FF, D_MODEL + PAD_COLS), jnp.bfloat16),
            pltpu.VMEM((2, GU_STAGE_ROWS, 2 * D_FF), jnp.float32),
            pltpu.VMEM((2, DOWN_STAGE_ROWS, D_MODEL), jnp.float32),
            pltpu.SemaphoreType.DMA((2,)),
            pltpu.SemaphoreType.DMA((2,)),
            pltpu.VMEM((FFN_TOKENS, D_FF), jnp.bfloat16),
        ],
        compiler_params=pltpu.CompilerParams(
            dimension_semantics=("arbitrary",), vmem_limit_bytes=VMEM_LIMIT_BYTES),
    )(h2d, pre_g, w_gu, w_down, post_g)


def _mixer_kernel(sinks_ref, h_ref, cos_ref, sin_ref, pre_g_ref, w_in_ref, conv_w_ref,
                  conv_b_ref, w_gate_ref, b_a_ref, b_x_ref, lam_ref, w_proj_ref, post_g_ref, o_ref,
                  un_ref, unf_ref, unp_ref, xhist_ref, hstate_ref, q_ref, kall_ref, vall_ref, ylru_ref,
                  yattn_ref, sgl_ref, sga_ref, plru_ref, merged_ref):
    ts = MIX_TOKENS
    seg_rows = ts // SEG_LEN
    t_idx = pl.program_id(1)
    f32, bf16 = jnp.float32, jnp.bfloat16

    @pl.when(t_idx == 0)
    def _():
        xhist_ref[...] = jnp.zeros_like(xhist_ref)
        hstate_ref[...] = jnp.zeros_like(hstate_ref)
        kall_ref[0:BLOCK, :] = jnp.zeros((BLOCK, KV_SLAB_WIDTH), bf16)
        vall_ref[0:BLOCK, :] = jnp.zeros((BLOCK, KV_SLAB_WIDTH), bf16)

    un = _rms_norm(h_ref[...], pre_g_ref[...])
    un_ref[...] = un.astype(bf16)
    for lg in range(D_MODEL // LANES):
        unf_ref[lg] = un[:, lg * LANES:(lg + 1) * LANES]
    for lg in range(D_MODEL // LANES):
        pieces = [unf_ref.at[lg][pl.ds(SEG_LEN * SUBLANES * grp + i, SUBLANES, stride=SEG_LEN), :]
                  for i in range(SEG_LEN) for grp in range(seg_rows // SUBLANES)]
        unp_ref[:, lg * LANES:(lg + 1) * LANES] = jnp.concatenate(pieces, axis=0).astype(bf16)

    def proj(lo, src_ref=un_ref):
        return jnp.dot(src_ref[...], w_in_ref[:, lo:lo + GROUP_W], preferred_element_type=f32)

    lane = lax.broadcasted_iota(jnp.int32, (ts, LANES), 1)
    first_half = (lane % HEAD_DIM) < HALF
    first_head = lane < HEAD_DIM

    def rope(x, c, s):
        partner = jnp.where(first_half, pltpu.roll(x, LANES - HALF, axis=1),
                            pltpu.roll(x, HALF, axis=1))
        return x * c + partner * s

    cos = cos_ref[...]
    sin = sin_ref[...]
    q_scale = (HEAD_DIM ** -0.5) * LOG2E
    cos_q = cos * q_scale
    sin_q = sin * q_scale
    seg_row = lax.broadcasted_iota(jnp.int32, (seg_rows, GROUP_W), 0)

    def shift_rows(cur, first_row):
        return jnp.where(seg_row == 0, first_row, pltpu.roll(cur, 1, axis=0))

    kv_nat = proj(C_K)
    k_nat, v_nat = kv_nat[:, :KV_WIDTH], kv_nat[:, KV_WIDTH:]
    for jj in range(N_KV_HEADS // 2):
        kr = rope(k_nat[:, jj * LANES:(jj + 1) * LANES], cos, sin)
        vv = v_nat[:, jj * LANES:(jj + 1) * LANES]
        kr_swapped = pltpu.roll(kr, HEAD_DIM, axis=1)
        vv_swapped = pltpu.roll(vv, HEAD_DIM, axis=1)
        for half in range(2):
            j = 2 * jj + half
            for dst_ref, same, swapped, fill_low, fill_high in (
                    (kall_ref, kr, kr_swapped, 0.0, 0.0),
                    (vall_ref, vv, vv_swapped, (lane == SUM_LANE_LOW).astype(f32),
                     (lane == SUM_LANE_HIGH).astype(f32))):
                low, high = (same, swapped) if half == 0 else (swapped, same)
                dst_ref[BLOCK:BLOCK + ts, (2 * j) * LANES:(2 * j + 1) * LANES] = (
                    jnp.where(first_head, low, fill_low).astype(bf16))
                dst_ref[BLOCK:BLOCK + ts, (2 * j + 1) * LANES:(2 * j + 2) * LANES] = (
                    jnp.where(first_head, fill_high, high).astype(bf16))

    for g in range(GROUPS):
        qg = proj(C_Q + g * GROUP_W)
        for pp in range(GROUP_W // LANES):
            lo = g * GROUP_W + pp * LANES
            q_ref[:, lo:lo + LANES] = rope(qg[:, pp * LANES:(pp + 1) * LANES], cos_q, sin_q).astype(bf16)

    def lru_stages(g):
        cg = slice(g * GROUP_W, (g + 1) * GROUP_W)
        st = {}

        def conv_and_gates():
            xg = proj(C_X + g * GROUP_W, unp_ref)
            x_blk = [xg[i * seg_rows:(i + 1) * seg_rows, :] for i in range(SEG_LEN)]
            wrapped = {}
            for i_src in range(SEG_LEN - CONV_WIDTH + 1, SEG_LEN):
                k = i_src - (SEG_LEN - CONV_WIDTH + 1)
                wrapped[i_src] = shift_rows(x_blk[i_src], xhist_ref[k, :, cg])
                xhist_ref[k, :, cg] = x_blk[i_src][seg_rows - 1:seg_rows, :]
            xc_blk = []
            for i in range(SEG_LEN):
                acc = conv_b_ref[:, cg]
                for d in range(CONV_WIDTH):
                    src = x_blk[i - d] if i >= d else wrapped[i - d + SEG_LEN]
                    acc = acc + src * conv_w_ref[CONV_WIDTH - 1 - d:CONV_WIDTH - d, cg]
                xc_blk.append(acc)
            st["xc"] = jnp.concatenate(xc_blk, axis=0)
            xc_b = st["xc"].astype(bf16)
            st["gates"] = [
                jnp.dot(xc_b[:, k * MXU_DIM:(k + 1) * MXU_DIM], w_gate_ref[g * (GROUP_W // MXU_DIM) + k],
                        preferred_element_type=f32) for k in range(GROUP_W // MXU_DIM)]

        def decay_and_input():
            gates, xc = st["gates"], st["xc"]
            th_r = jnp.tanh(jnp.concatenate([ri[:, :MXU_DIM] for ri in gates], axis=1)
                            + 0.5 * b_a_ref[:, cg])
            th_i = jnp.tanh(jnp.concatenate([ri[:, MXU_DIM:] for ri in gates], axis=1)
                            + 0.5 * b_x_ref[:, cg])
            lam = lam_ref[:, cg]
            softplus_neg_lam = jnp.maximum(-lam, 0.0) + jnp.log1p(jnp.exp(-jnp.abs(lam)))
            k = (0.5 * LRU_C) * softplus_neg_lam
            neg_log_a = th_r * k + k
            a = jnp.exp2(neg_log_a * (-LOG2E))
            t = jnp.tanh(neg_log_a) * (a * a + 1.0)
            mult = jnp.where(t > 0.0, t * lax.rsqrt(t), 0.0)
            half_xc = 0.5 * xc
            st["a"], st["b"] = a, mult * (th_i * half_xc + half_xc)

        def local_scan():
            a, b = st["a"], st["b"]
            a_blk = [a[i * seg_rows:(i + 1) * seg_rows, :] for i in range(SEG_LEN)]
            b_blk = [b[i * seg_rows:(i + 1) * seg_rows, :] for i in range(SEG_LEN)]
            h_loc, a_cum = [b_blk[0]], [a_blk[0]]
            for i in range(1, SEG_LEN):
                h_loc.append(a_blk[i] * h_loc[-1] + b_blk[i])
                a_cum.append(a_blk[i] * a_cum[-1])
            st["h_loc"], st["a_cum"] = h_loc, a_cum

        def carry_and_gate():
            h_loc, a_cum = st["h_loc"], st["a_cum"]
            seg_a, seg_h = a_cum[-1], h_loc[-1]
            shift = 1
            while shift < SUBLANES:
                keep = (seg_row % SUBLANES) >= shift
                a_prev = jnp.where(keep, pltpu.roll(seg_a, shift, axis=0), 1.0)
                h_prev = jnp.where(keep, pltpu.roll(seg_h, shift, axis=0), 0.0)
                seg_h = seg_a * h_prev + seg_h
                seg_a = seg_a * a_prev
                shift *= 2
            carry = hstate_ref[:, cg]
            state_in = carry
            ends = []
            for v in range(seg_rows // SUBLANES):
                lo = v * SUBLANES
                hv = seg_a[lo:lo + SUBLANES, :] * carry + seg_h[lo:lo + SUBLANES, :]
                ends.append(hv)
                carry = hv[SUBLANES - 1:SUBLANES, :]
            hstate_ref[:, cg] = carry
            seg_in = shift_rows(jnp.concatenate(ends, axis=0), state_in)
            h_lru = jnp.concatenate([h_loc[i] + a_cum[i] * seg_in for i in range(SEG_LEN)], axis=0)
            ylru_ref[:, cg] = (h_lru * jax.nn.gelu(proj(C_GATE + g * GROUP_W, unp_ref))).astype(bf16)

        def merge_gates():
            sgl_ref[:, cg] = _sigmoid(proj(C_GL + g * GROUP_W))
            sga_ref[:, cg] = _sigmoid(proj(C_GA + g * GROUP_W))

        return [conv_and_gates, decay_and_input, local_scan, carry_and_gate, merge_gates]

    first_head2 = lax.broadcasted_iota(jnp.int32, (2 * BLOCK, LANES), 1) < HEAD_DIM
    qi = lax.broadcasted_iota(jnp.int32, (2 * BLOCK, 2 * BLOCK), 0) % BLOCK
    si = lax.broadcasted_iota(jnp.int32, (2 * BLOCK, 2 * BLOCK), 1)
    diff = BLOCK + qi - si
    band = (diff >= 0) & (diff < WINDOW)
    first_pair_rows = lax.broadcasted_iota(jnp.int32, (2 * BLOCK, 1), 0) < BLOCK

    def scores(j, qb):
        r0 = qb * BLOCK
        q2 = jnp.concatenate(
            [q_ref[r0:r0 + BLOCK, (2 * j + pp) * LANES:(2 * j + pp + 1) * LANES] for pp in range(2)],
            axis=0)
        mask = band & ((si >= BLOCK) | (t_idx * (ts // BLOCK) + qb > 0))
        halves = []
        for second in range(2):
            slab = slice((2 * j + second) * LANES, (2 * j + second + 1) * LANES)
            head = 4 * j + second
            sink = jnp.where(first_pair_rows, sinks_ref[head] * LOG2E, sinks_ref[head + 2] * LOG2E)
            s = lax.dot_general(q2, kall_ref[r0:r0 + 2 * BLOCK, slab], (((1,), (1,)), ((), ())),
                                preferred_element_type=f32)
            s = jnp.where(mask, s, MASK_VALUE)
            m = jnp.maximum(jnp.max(s, axis=1, keepdims=True), sink)
            halves.append((s, m, sink))
        return halves

    def weighted_values(j, qb, halves):
        r0 = qb * BLOCK
        parts = []
        for second, (s, m, sink) in enumerate(halves):
            slab = slice((2 * j + second) * LANES, (2 * j + second + 1) * LANES)
            e = jnp.exp2(s - m).astype(bf16)
            pv = jnp.dot(e, vall_ref[r0:r0 + 2 * BLOCK, slab], preferred_element_type=f32)
            sum_lane = SUM_LANE_HIGH if second else SUM_LANE_LOW
            denom = pv[:, sum_lane:sum_lane + 1] + jnp.exp2(sink - m)
            parts.append(pv * (1.0 / denom))
        out = jnp.where(first_head2, parts[0], parts[1]).astype(bf16)
        for pp in range(2):
            lo = (2 * j + pp) * LANES
            yattn_ref[r0:r0 + BLOCK, lo:lo + LANES] = out[pp * BLOCK:(pp + 1) * BLOCK, :]

    def attention_stages():
        blocks = [(j, qb) for j in range(N_KV_HEADS) for qb in range(ts // BLOCK)]
        pending = {}

        def stage(n):
            def run():
                if n < len(blocks):
                    pending[n] = scores(*blocks[n])
                if n >= 1:
                    weighted_values(*blocks[n - 1], pending.pop(n - 1))
            return run

        return [stage(n) for n in range(len(blocks) + 1)]

    def lru_projection(c):
        p_lru = jnp.dot(ylru_ref[...], w_proj_ref[:, P_LRU + c * GROUP_W:P_LRU + (c + 1) * GROUP_W],
                        preferred_element_type=f32)
        for half in range(GROUP_W // LANES):
            lg = c * (GROUP_W // LANES) + half
            for i in range(SEG_LEN):
                for grp in range(seg_rows // SUBLANES):
                    row = i * seg_rows + grp * SUBLANES
                    plru_ref.at[lg][pl.ds(SEG_LEN * SUBLANES * grp + i, SUBLANES, stride=SEG_LEN), :] = (
                        p_lru[row:row + SUBLANES, half * LANES:(half + 1) * LANES])

    lru_list = [stage for g in range(GROUPS) for stage in lru_stages(g)]
    lru_list += [lambda c=c: lru_projection(c) for c in range(GROUPS)]
    att_list = attention_stages()
    att_per_lru = -(-len(att_list) // len(lru_list))
    while lru_list or att_list:
        if lru_list:
            lru_list.pop(0)()
        for _ in range(att_per_lru):
            if att_list:
                att_list.pop(0)()

    kall_ref[0:BLOCK, :] = kall_ref[ts:ts + BLOCK, :]
    vall_ref[0:BLOCK, :] = vall_ref[ts:ts + BLOCK, :]

    for n in range(GROUPS):
        cn = slice(n * GROUP_W, (n + 1) * GROUP_W)
        p_attn = jnp.dot(yattn_ref[...], w_proj_ref[:, P_ATTN + n * GROUP_W:P_ATTN + (n + 1) * GROUP_W],
                         preferred_element_type=f32)
        p_lru = jnp.concatenate([plru_ref[n * (GROUP_W // LANES) + half] for half in range(GROUP_W // LANES)],
                                axis=1)
        merged_ref[:, cn] = (sgl_ref[:, cn] * p_lru + sga_ref[:, cn] * p_attn).astype(bf16)
    m_out = [jnp.dot(merged_ref[...], w_proj_ref[:, P_OUT + n * GROUP_W:P_OUT + (n + 1) * GROUP_W],
                     preferred_element_type=f32) for n in range(GROUPS)]
    mean_sq = sum(jnp.sum(mo * mo, axis=1, keepdims=True) for mo in m_out) * (1.0 / D_MODEL)
    inv_rms = lax.rsqrt(mean_sq + NORM_EPS)
    for n in range(GROUPS):
        cn = slice(n * GROUP_W, (n + 1) * GROUP_W)
        o_ref[:, cn] = h_ref[:, cn] + m_out[n] * inv_rms * post_g_ref[:, cn]


def _mixer(h3d, sinks, cos_t, sin_t, pre_g, w_in_p, conv_w, conv_b, w_gate, b_a, b_x, lam,
           w_proj, post_g):
    bsz, s, _ = h3d.shape
    ts = MIX_TOKENS
    tile = pl.BlockSpec((None, ts, D_MODEL), lambda b, t: (b, t, 0))
    rope_spec = pl.BlockSpec((ts, LANES), lambda b, t: (t, 0))
    return pl.pallas_call(
        _mixer_kernel,
        name="mixer",
        grid=(bsz, s // ts),
        in_specs=[
            pl.BlockSpec(memory_space=pltpu.SMEM),
            tile, rope_spec, rope_spec,
            _const_spec((1, D_MODEL)),
            _const_spec((D_MODEL, IN_WIDTH)),
            _const_spec((CONV_WIDTH, LRU_WIDTH)),
            _const_spec((1, LRU_WIDTH)),
            _const_spec((LRU_WIDTH // MXU_DIM, MXU_DIM, 2 * MXU_DIM)),
            _const_spec((1, LRU_WIDTH)),
            _const_spec((1, LRU_WIDTH)),
            _const_spec((1, LRU_WIDTH)),
            _const_spec((D_MODEL, 3 * D_MODEL + PAD_COLS)),
            _const_spec((1, D_MODEL)),
        ],
        out_specs=tile,
        out_shape=jax.ShapeDtypeStruct(h3d.shape, jnp.float32),
        scratch_shapes=[
            pltpu.VMEM((ts, D_MODEL), jnp.bfloat16),
            pltpu.VMEM((D_MODEL // LANES, ts, LANES), jnp.float32),
            pltpu.VMEM((ts, D_MODEL), jnp.bfloat16),
            pltpu.VMEM((CONV_WIDTH - 1, 1, LRU_WIDTH), jnp.float32),
            pltpu.VMEM((1, LRU_WIDTH), jnp.float32),
            pltpu.VMEM((ts, ATTN_WIDTH), jnp.bfloat16),
            pltpu.VMEM((BLOCK + ts, KV_SLAB_WIDTH), jnp.bfloat16),
            pltpu.VMEM((BLOCK + ts, KV_SLAB_WIDTH), jnp.bfloat16),
            pltpu.VMEM((ts, LRU_WIDTH), jnp.bfloat16),
            pltpu.VMEM((ts, ATTN_WIDTH), jnp.bfloat16),
            pltpu.VMEM((ts, D_MODEL), jnp.float32),
            pltpu.VMEM((ts, D_MODEL), jnp.float32),
            pltpu.VMEM((D_MODEL // LANES, ts, LANES), jnp.float32),
            pltpu.VMEM((ts, D_MODEL), jnp.bfloat16),
        ],
        compiler_params=pltpu.CompilerParams(
            dimension_semantics=("arbitrary", "arbitrary"), vmem_limit_bytes=VMEM_LIMIT_BYTES),
    )(sinks, h3d, cos_t, sin_t, pre_g, w_in_p, conv_w, conv_b, w_gate, b_a, b_x, lam,
      w_proj, post_g)


def _pad_cast(*ws):
    zeros = jnp.zeros((ws[0].shape[0], PAD_COLS), jnp.bfloat16)
    return jnp.concatenate([w.astype(jnp.bfloat16) for w in ws] + [zeros], axis=1)


def _gate_weights(w_a, w_x):
    per = MXU_DIM // LRU_BLOCK_W
    eye = jnp.eye(per, dtype=w_a.dtype)

    def expand(w):
        w4 = w.reshape(LRU_BLOCKS // per, per, LRU_BLOCK_W, LRU_BLOCK_W)
        return jnp.einsum("gncd,nm->gncmd", w4, eye).reshape(LRU_BLOCKS // per, MXU_DIM, MXU_DIM)

    return (0.5 * jnp.concatenate([expand(w_a), expand(w_x)], axis=2)).astype(jnp.bfloat16)


def _rope_tables(seq_len):
    inv_freq = ROPE_THETA ** (-np.arange(HALF, dtype=np.float64) / HALF)
    ang = np.arange(seq_len, dtype=np.float64)[:, None] * inv_freq[None, :]
    cos, sin = np.cos(ang), np.sin(ang)
    return (jnp.asarray(np.concatenate([cos, cos, cos, cos], axis=1), jnp.float32),
            jnp.asarray(np.concatenate([-sin, sin, -sin, sin], axis=1), jnp.float32))


def kernel(x, ffn1_pre_g, ffn1_w_gu, ffn1_w_down, ffn1_post_g, mix_pre_g, w_in, conv_w, conv_b, lru_w_a, lru_b_a, lru_w_x, lru_b_x, lru_lambda, attn_sinks, w_proj_lru, w_proj_attn, w_out, mix_post_g, ffn2_pre_g, ffn2_w_gu, ffn2_w_down, ffn2_post_g):
    bsz, s, d = x.shape
    depth = ffn1_pre_g.shape[0]
    cos_t, sin_t = _rope_tables(s)
    bf = jnp.bfloat16
    h = x
    for l in range(depth):
        h = _ffn(h.reshape(bsz * s, d), ffn1_pre_g[l][None], ffn1_w_gu[l], ffn1_w_down[l],
                 ffn1_post_g[l][None]).reshape(bsz, s, d)
        h = _mixer(h, attn_sinks[l], cos_t, sin_t, mix_pre_g[l][None], w_in[l].astype(bf),
                   conv_w[l], conv_b[l][None], _gate_weights(lru_w_a[l], lru_w_x[l]),
                   lru_b_a[l][None], lru_b_x[l][None], lru_lambda[l][None],
                   _pad_cast(w_proj_lru[l], w_proj_attn[l], w_out[l]),
                   mix_post_g[l][None])
        h = _ffn(h.reshape(bsz * s, d), ffn2_pre_g[l][None], ffn2_w_gu[l], ffn2_w_down[l],
                 ffn2_post_g[l][None]).reshape(bsz, s, d)
    return h
```

```python
import jax
import jax.numpy as jnp
import numpy as np
from jax import lax
from jax.experimental import pallas as pl
from jax.experimental.pallas import tpu as pltpu

D_MODEL = 1024
LRU_WIDTH = 1024
LRU_BLOCKS = 16
LRU_BLOCK_W = LRU_WIDTH // LRU_BLOCKS
CONV_WIDTH = 4
LRU_C = 8.0
HEAD_DIM = 64
HALF = HEAD_DIM // 2
N_Q_HEADS = 16
N_KV_HEADS = 4
ATTN_WIDTH = N_Q_HEADS * HEAD_DIM
KV_WIDTH = N_KV_HEADS * HEAD_DIM
WINDOW = 128
BLOCK = 128
ROPE_THETA = 10000.0
D_FF = 2816
MACARON_SCALE = 0.5
NORM_EPS = 1e-6
MASK_VALUE = -1e30

LANES = 128
SUBLANES = 8
MXU_DIM = 256
VMEM_LIMIT_BYTES = 56 * 1024 * 1024

FFN_TOKENS = 512
FFN_CHUNK = 256
DOWN_CAST_ROWS = 176
MIX_TOKENS = 512
SEG_LEN = SUBLANES

GROUP_W = 2 * MXU_DIM
GROUPS = LRU_WIDTH // GROUP_W
C_GATE = 0
C_X = C_GATE + LRU_WIDTH
C_Q = C_X + LRU_WIDTH
C_K = C_Q + ATTN_WIDTH
C_V = C_K + KV_WIDTH
C_GL = C_V + KV_WIDTH
C_GA = C_GL + D_MODEL
IN_WIDTH = C_GA + D_MODEL
LOG2E = 1.4426950408889634
P_LRU, P_ATTN, P_OUT = 0, D_MODEL, 2 * D_MODEL
KV_SLAB_WIDTH = N_KV_HEADS * 2 * LANES
SUM_LANE_LOW, SUM_LANE_HIGH = HEAD_DIM, 0
PAD_COLS = LANES


def _rms_norm(x, g):
    return x * lax.rsqrt(jnp.mean(x * x, axis=-1, keepdims=True) + NORM_EPS) * g


def _sigmoid(x):
    return 0.5 * jnp.tanh(0.5 * x) + 0.5


def _const_spec(shape):
    n = len(shape)
    return pl.BlockSpec(shape, lambda *_: (0,) * n, pipeline_mode=pl.Buffered(1))


def _ffn_kernel(h_ref, pre_g_ref, w_gu_hbm, w_down_hbm, post_g_ref, o_ref,
                w_gu_ref, w_down_ref, stage_gu_ref, stage_down_ref, sem_gu_ref, sem_down_ref, act_ref):
    bf16, f32 = jnp.bfloat16, jnp.float32
    n_chunks = D_FF // FFN_CHUNK

    def gu_copies(c):
        slot = c % 2
        return [pltpu.make_async_copy(w_gu_hbm.at[:, pl.ds(part * D_FF + c * FFN_CHUNK, FFN_CHUNK)],
                                      stage_gu_ref.at[slot, part], sem_gu_ref.at[2 * slot + part])
                for part in range(2)]

    down_copy = pltpu.make_async_copy(w_down_hbm, stage_down_ref, sem_down_ref.at[0])

    def body(stage_weights):
        if stage_weights:
            down_copy.start()
            for cp in gu_copies(0):
                cp.start()
        xn = _rms_norm(h_ref[...], pre_g_ref[...]).astype(bf16)
        for c in range(n_chunks):
            lo = c * FFN_CHUNK
            if stage_weights:
                if c + 1 < n_chunks:
                    for cp in gu_copies(c + 1):
                        cp.start()
                for part, cp in enumerate(gu_copies(c)):
                    cp.wait()
                    col = part * D_FF + lo
                    w_gu_ref[:, col:col + FFN_CHUNK] = stage_gu_ref[c % 2, part].astype(bf16)
            g = jnp.dot(xn, w_gu_ref[:, lo:lo + FFN_CHUNK], preferred_element_type=f32)
            u = jnp.dot(xn, w_gu_ref[:, D_FF + lo:D_FF + lo + FFN_CHUNK], preferred_element_type=f32)
            act_ref[:, lo:lo + FFN_CHUNK] = (jax.nn.silu(g) * u).astype(bf16)
        if stage_weights:
            down_copy.wait()
            for r in range(0, D_FF, DOWN_CAST_ROWS):
                w_down_ref[r:r + DOWN_CAST_ROWS, 0:D_MODEL] = (
                    stage_down_ref[r:r + DOWN_CAST_ROWS, :].astype(bf16))
        f = [jnp.dot(act_ref[...], w_down_ref[:, n * GROUP_W:(n + 1) * GROUP_W],
                     preferred_element_type=f32) for n in range(D_MODEL // GROUP_W)]
        mean_sq = sum(jnp.sum(fn * fn, axis=1, keepdims=True) for fn in f) * (1.0 / D_MODEL)
        scale = MACARON_SCALE * lax.rsqrt(mean_sq + NORM_EPS)
        for n in range(D_MODEL // GROUP_W):
            cn = slice(n * GROUP_W, (n + 1) * GROUP_W)
            o_ref[:, cn] = h_ref[:, cn] + f[n] * scale * post_g_ref[:, cn]

    first = pl.program_id(0) == 0
    pl.when(first)(lambda: body(True))
    pl.when(jnp.logical_not(first))(lambda: body(False))


def _ffn(h2d, pre_g, w_gu, w_down, post_g):
    m = h2d.shape[0]
    tile = pl.BlockSpec((FFN_TOKENS, D_MODEL), lambda i: (i, 0))
    return pl.pallas_call(
        _ffn_kernel,
        name="ffn",
        grid=(m // FFN_TOKENS,),
        in_specs=[
            tile,
            _const_spec((1, D_MODEL)),
            pl.BlockSpec(memory_space=pl.ANY),
            pl.BlockSpec(memory_space=pl.ANY),
            _const_spec((1, D_MODEL)),
        ],
        out_specs=tile,
        out_shape=jax.ShapeDtypeStruct((m, D_MODEL), jnp.float32),
        scratch_shapes=[
            pltpu.VMEM((D_MODEL, 2 * D_FF), jnp.bfloat16),
            pltpu.VMEM((D_FF, D_MODEL + PAD_COLS), jnp.bfloat16),
            pltpu.VMEM((2, 2, D_MODEL, FFN_CHUNK), jnp.float32),
            pltpu.VMEM((D_FF, D_MODEL), jnp.float32),
            pltpu.SemaphoreType.DMA((4,)),
            pltpu.SemaphoreType.DMA((1,)),
            pltpu.VMEM((FFN_TOKENS, D_FF), jnp.bfloat16),
        ],
        compiler_params=pltpu.CompilerParams(
            dimension_semantics=("arbitrary",), vmem_limit_bytes=VMEM_LIMIT_BYTES),
    )(h2d, pre_g, w_gu, w_down, post_g)


def _mixer_kernel(sinks_ref, h_ref, cos_ref, sin_ref, pre_g_ref, w_in_ref, conv_w_ref,
                  conv_b_ref, w_gate_ref, b_a_ref, b_x_ref, lam_ref, w_proj_ref, post_g_ref, o_ref,
                  un_ref, unf_ref, unp_ref, xhist_ref, hstate_ref, q_ref, kall_ref, vall_ref, ylru_ref,
                  yattn_ref, sgl_ref, sga_ref, plru_ref, merged_ref):
    ts = MIX_TOKENS
    seg_rows = ts // SEG_LEN
    t_idx = pl.program_id(1)
    f32, bf16 = jnp.float32, jnp.bfloat16

    @pl.when(t_idx == 0)
    def _():
        xhist_ref[...] = jnp.zeros_like(xhist_ref)
        hstate_ref[...] = jnp.zeros_like(hstate_ref)
        kall_ref[0:BLOCK, :] = jnp.zeros((BLOCK, KV_SLAB_WIDTH), bf16)
        vall_ref[0:BLOCK, :] = jnp.zeros((BLOCK, KV_SLAB_WIDTH), bf16)

    un = _rms_norm(h_ref[...], pre_g_ref[...])
    un_ref[...] = un.astype(bf16)
    for lg in range(D_MODEL // LANES):
        unf_ref[lg] = un[:, lg * LANES:(lg + 1) * LANES]
    for lg in range(D_MODEL // LANES):
        pieces = [unf_ref.at[lg][pl.ds(SEG_LEN * SUBLANES * grp + i, SUBLANES, stride=SEG_LEN), :]
                  for i in range(SEG_LEN) for grp in range(seg_rows // SUBLANES)]
        unp_ref[:, lg * LANES:(lg + 1) * LANES] = jnp.concatenate(pieces, axis=0).astype(bf16)

    def proj(lo, src_ref=un_ref):
        return jnp.dot(src_ref[...], w_in_ref[:, lo:lo + GROUP_W], preferred_element_type=f32)

    lane = lax.broadcasted_iota(jnp.int32, (ts, LANES), 1)
    first_half = (lane % HEAD_DIM) < HALF
    first_head = lane < HEAD_DIM

    def rope(x, c, s):
        partner = jnp.where(first_half, pltpu.roll(x, LANES - HALF, axis=1),
                            pltpu.roll(x, HALF, axis=1))
        return x * c + partner * s

    cos = cos_ref[...]
    sin = sin_ref[...]
    q_scale = (HEAD_DIM ** -0.5) * LOG2E
    cos_q = cos * q_scale
    sin_q = sin * q_scale
    seg_row = lax.broadcasted_iota(jnp.int32, (seg_rows, GROUP_W), 0)

    def shift_rows(cur, first_row):
        return jnp.where(seg_row == 0, first_row, pltpu.roll(cur, 1, axis=0))

    kv_nat = proj(C_K)
    k_nat, v_nat = kv_nat[:, :KV_WIDTH], kv_nat[:, KV_WIDTH:]
    for jj in range(N_KV_HEADS // 2):
        kr = rope(k_nat[:, jj * LANES:(jj + 1) * LANES], cos, sin)
        vv = v_nat[:, jj * LANES:(jj + 1) * LANES]
        kr_swapped = pltpu.roll(kr, HEAD_DIM, axis=1)
        vv_swapped = pltpu.roll(vv, HEAD_DIM, axis=1)
        for half in range(2):
            j = 2 * jj + half
            for dst_ref, same, swapped, fill_low, fill_high in (
                    (kall_ref, kr, kr_swapped, 0.0, 0.0),
                    (vall_ref, vv, vv_swapped, (lane == SUM_LANE_LOW).astype(f32),
                     (lane == SUM_LANE_HIGH).astype(f32))):
                low, high = (same, swapped) if half == 0 else (swapped, same)
                dst_ref[BLOCK:BLOCK + ts, (2 * j) * LANES:(2 * j + 1) * LANES] = (
                    jnp.where(first_head, low, fill_low).astype(bf16))
                dst_ref[BLOCK:BLOCK + ts, (2 * j + 1) * LANES:(2 * j + 2) * LANES] = (
                    jnp.where(first_head, fill_high, high).astype(bf16))

    for g in range(GROUPS):
        qg = proj(C_Q + g * GROUP_W)
        for pp in range(GROUP_W // LANES):
            lo = g * GROUP_W + pp * LANES
            q_ref[:, lo:lo + LANES] = rope(qg[:, pp * LANES:(pp + 1) * LANES], cos_q, sin_q).astype(bf16)

    def lru_stages(g):
        cg = slice(g * GROUP_W, (g + 1) * GROUP_W)
        st = {}

        def conv_and_gates():
            xg = proj(C_X + g * GROUP_W, unp_ref)
            x_blk = [xg[i * seg_rows:(i + 1) * seg_rows, :] for i in range(SEG_LEN)]
            wrapped = {}
            for i_src in range(SEG_LEN - CONV_WIDTH + 1, SEG_LEN):
                k = i_src - (SEG_LEN - CONV_WIDTH + 1)
                wrapped[i_src] = shift_rows(x_blk[i_src], xhist_ref[k, :, cg])
                xhist_ref[k, :, cg] = x_blk[i_src][seg_rows - 1:seg_rows, :]
            xc_blk = []
            for i in range(SEG_LEN):
                acc = conv_b_ref[:, cg]
                for d in range(CONV_WIDTH):
                    src = x_blk[i - d] if i >= d else wrapped[i - d + SEG_LEN]
                    acc = acc + src * conv_w_ref[CONV_WIDTH - 1 - d:CONV_WIDTH - d, cg]
                xc_blk.append(acc)
            st["xc"] = jnp.concatenate(xc_blk, axis=0)
            xc_b = st["xc"].astype(bf16)
            st["gates"] = [
                jnp.dot(xc_b[:, k * MXU_DIM:(k + 1) * MXU_DIM], w_gate_ref[g * (GROUP_W // MXU_DIM) + k],
                        preferred_element_type=f32) for k in range(GROUP_W // MXU_DIM)]

        def decay_and_input():
            gates, xc = st["gates"], st["xc"]
            th_r = jnp.tanh(jnp.concatenate([ri[:, :MXU_DIM] for ri in gates], axis=1)
                            + 0.5 * b_a_ref[:, cg])
            th_i = jnp.tanh(jnp.concatenate([ri[:, MXU_DIM:] for ri in gates], axis=1)
                            + 0.5 * b_x_ref[:, cg])
            lam = lam_ref[:, cg]
            softplus_neg_lam = jnp.maximum(-lam, 0.0) + jnp.log1p(jnp.exp(-jnp.abs(lam)))
            k = (0.5 * LRU_C) * softplus_neg_lam
            neg_log_a = th_r * k + k
            a = jnp.exp2(neg_log_a * (-LOG2E))
            t = jnp.tanh(neg_log_a) * (a * a + 1.0)
            mult = jnp.where(t > 0.0, t * lax.rsqrt(t), 0.0)
            half_xc = 0.5 * xc
            st["a"], st["b"] = a, mult * (th_i * half_xc + half_xc)

        def local_scan():
            a, b = st["a"], st["b"]
            a_blk = [a[i * seg_rows:(i + 1) * seg_rows, :] for i in range(SEG_LEN)]
            b_blk = [b[i * seg_rows:(i + 1) * seg_rows, :] for i in range(SEG_LEN)]
            h_loc, a_cum = [b_blk[0]], [a_blk[0]]
            for i in range(1, SEG_LEN):
                h_loc.append(a_blk[i] * h_loc[-1] + b_blk[i])
                a_cum.append(a_blk[i] * a_cum[-1])
            st["h_loc"], st["a_cum"] = h_loc, a_cum

        def carry_and_gate():
            h_loc, a_cum = st["h_loc"], st["a_cum"]
            seg_a, seg_h = a_cum[-1], h_loc[-1]
            shift = 1
            while shift < SUBLANES:
                keep = (seg_row % SUBLANES) >= shift
                a_prev = jnp.where(keep, pltpu.roll(seg_a, shift, axis=0), 1.0)
                h_prev = jnp.where(keep, pltpu.roll(seg_h, shift, axis=0), 0.0)
                seg_h = seg_a * h_prev + seg_h
                seg_a = seg_a * a_prev
                shift *= 2
            carry = hstate_ref[:, cg]
            state_in = carry
            ends = []
            for v in range(seg_rows // SUBLANES):
                lo = v * SUBLANES
                hv = seg_a[lo:lo + SUBLANES, :] * carry + seg_h[lo:lo + SUBLANES, :]
                ends.append(hv)
                carry = hv[SUBLANES - 1:SUBLANES, :]
            hstate_ref[:, cg] = carry
            seg_in = shift_rows(jnp.concatenate(ends, axis=0), state_in)
            h_lru = jnp.concatenate([h_loc[i] + a_cum[i] * seg_in for i in range(SEG_LEN)], axis=0)
            ylru_ref[:, cg] = (h_lru * jax.nn.gelu(proj(C_GATE + g * GROUP_W, unp_ref))).astype(bf16)

        def merge_gates():
            sgl_ref[:, cg] = _sigmoid(proj(C_GL + g * GROUP_W))
            sga_ref[:, cg] = _sigmoid(proj(C_GA + g * GROUP_W))

        return [conv_and_gates, decay_and_input, local_scan, carry_and_gate, merge_gates]

    first_head2 = lax.broadcasted_iota(jnp.int32, (2 * BLOCK, LANES), 1) < HEAD_DIM
    qi = lax.broadcasted_iota(jnp.int32, (2 * BLOCK, 2 * BLOCK), 0) % BLOCK
    si = lax.broadcasted_iota(jnp.int32, (2 * BLOCK, 2 * BLOCK), 1)
    diff = BLOCK + qi - si
    band = (diff >= 0) & (diff < WINDOW)
    first_pair_rows = lax.broadcasted_iota(jnp.int32, (2 * BLOCK, 1), 0) < BLOCK

    def scores(j, qb):
        r0 = qb * BLOCK
        q2 = jnp.concatenate(
            [q_ref[r0:r0 + BLOCK, (2 * j + pp) * LANES:(2 * j + pp + 1) * LANES] for pp in range(2)],
            axis=0)
        mask = band & ((si >= BLOCK) | (t_idx * (ts // BLOCK) + qb > 0))
        halves = []
        for second in range(2):
            slab = slice((2 * j + second) * LANES, (2 * j + second + 1) * LANES)
            head = 4 * j + second
            sink = jnp.where(first_pair_rows, sinks_ref[head] * LOG2E, sinks_ref[head + 2] * LOG2E)
            s = lax.dot_general(q2, kall_ref[r0:r0 + 2 * BLOCK, slab], (((1,), (1,)), ((), ())),
                                preferred_element_type=f32)
            s = jnp.where(mask, s, MASK_VALUE)
            m = jnp.maximum(jnp.max(s, axis=1, keepdims=True), sink)
            halves.append((s, m, sink))
        return halves

    def weighted_values(j, qb, halves):
        r0 = qb * BLOCK
        parts = []
        for second, (s, m, sink) in enumerate(halves):
            slab = slice((2 * j + second) * LANES, (2 * j + second + 1) * LANES)
            e = jnp.exp2(s - m).astype(bf16)
            pv = jnp.dot(e, vall_ref[r0:r0 + 2 * BLOCK, slab], preferred_element_type=f32)
            sum_lane = SUM_LANE_HIGH if second else SUM_LANE_LOW
            denom = pv[:, sum_lane:sum_lane + 1] + jnp.exp2(sink - m)
            parts.append(pv * (1.0 / denom))
        out = jnp.where(first_head2, parts[0], parts[1]).astype(bf16)
        for pp in range(2):
            lo = (2 * j + pp) * LANES
            yattn_ref[r0:r0 + BLOCK, lo:lo + LANES] = out[pp * BLOCK:(pp + 1) * BLOCK, :]

    def attention_stages():
        blocks = [(j, qb) for j in range(N_KV_HEADS) for qb in range(ts // BLOCK)]
        pending = {}

        def stage(n):
            def run():
                if n < len(blocks):
                    pending[n] = scores(*blocks[n])
                if n >= 1:
                    weighted_values(*blocks[n - 1], pending.pop(n - 1))
            return run

        return [stage(n) for n in range(len(blocks) + 1)]

    def lru_projection(c):
        p_lru = jnp.dot(ylru_ref[...], w_proj_ref[:, P_LRU + c * GROUP_W:P_LRU + (c + 1) * GROUP_W],
                        preferred_element_type=f32)
        for half in range(GROUP_W // LANES):
            lg = c * (GROUP_W // LANES) + half
            for i in range(SEG_LEN):
                for grp in range(seg_rows // SUBLANES):
                    row = i * seg_rows + grp * SUBLANES
                    plru_ref.at[lg][pl.ds(SEG_LEN * SUBLANES * grp + i, SUBLANES, stride=SEG_LEN), :] = (
                        p_lru[row:row + SUBLANES, half * LANES:(half + 1) * LANES])

    lru_list = [stage for g in range(GROUPS) for stage in lru_stages(g)]
    lru_list += [lambda c=c: lru_projection(c) for c in range(GROUPS)]
    att_list = attention_stages()
    att_per_lru = -(-len(att_list) // len(lru_list))
    while lru_list or att_list:
        if lru_list:
            lru_list.pop(0)()
        for _ in range(att_per_lru):
            if att_list:
                att_list.pop(0)()

    kall_ref[0:BLOCK, :] = kall_ref[ts:ts + BLOCK, :]
    vall_ref[0:BLOCK, :] = vall_ref[ts:ts + BLOCK, :]

    for n in range(GROUPS):
        cn = slice(n * GROUP_W, (n + 1) * GROUP_W)
        p_attn = jnp.dot(yattn_ref[...], w_proj_ref[:, P_ATTN + n * GROUP_W:P_ATTN + (n + 1) * GROUP_W],
                         preferred_element_type=f32)
        p_lru = jnp.concatenate([plru_ref[n * (GROUP_W // LANES) + half] for half in range(GROUP_W // LANES)],
                                axis=1)
        merged_ref[:, cn] = (sgl_ref[:, cn] * p_lru + sga_ref[:, cn] * p_attn).astype(bf16)
    m_out = [jnp.dot(merged_ref[...], w_proj_ref[:, P_OUT + n * GROUP_W:P_OUT + (n + 1) * GROUP_W],
                     preferred_element_type=f32) for n in range(GROUPS)]
    mean_sq = sum(jnp.sum(mo * mo, axis=1, keepdims=True) for mo in m_out) * (1.0 / D_MODEL)
    inv_rms = lax.rsqrt(mean_sq + NORM_EPS)
    for n in range(GROUPS):
        cn = slice(n * GROUP_W, (n + 1) * GROUP_W)
        o_ref[:, cn] = h_ref[:, cn] + m_out[n] * inv_rms * post_g_ref[:, cn]


def _mixer(h3d, sinks, cos_t, sin_t, pre_g, w_in_p, conv_w, conv_b, w_gate, b_a, b_x, lam,
           w_proj, post_g):
    bsz, s, _ = h3d.shape
    ts = MIX_TOKENS
    tile = pl.BlockSpec((None, ts, D_MODEL), lambda b, t: (b, t, 0))
    rope_spec = pl.BlockSpec((ts, LANES), lambda b, t: (t, 0))
    return pl.pallas_call(
        _mixer_kernel,
        name="mixer",
        grid=(bsz, s // ts),
        in_specs=[
            pl.BlockSpec(memory_space=pltpu.SMEM),
            tile, rope_spec, rope_spec,
            _const_spec((1, D_MODEL)),
            _const_spec((D_MODEL, IN_WIDTH)),
            _const_spec((CONV_WIDTH, LRU_WIDTH)),
            _const_spec((1, LRU_WIDTH)),
            _const_spec((LRU_WIDTH // MXU_DIM, MXU_DIM, 2 * MXU_DIM)),
            _const_spec((1, LRU_WIDTH)),
            _const_spec((1, LRU_WIDTH)),
            _const_spec((1, LRU_WIDTH)),
            _const_spec((D_MODEL, 3 * D_MODEL + PAD_COLS)),
            _const_spec((1, D_MODEL)),
        ],
        out_specs=tile,
        out_shape=jax.ShapeDtypeStruct(h3d.shape, jnp.float32),
        scratch_shapes=[
            pltpu.VMEM((ts, D_MODEL), jnp.bfloat16),
            pltpu.VMEM((D_MODEL // LANES, ts, LANES), jnp.float32),
            pltpu.VMEM((ts, D_MODEL), jnp.bfloat16),
            pltpu.VMEM((CONV_WIDTH - 1, 1, LRU_WIDTH), jnp.float32),
            pltpu.VMEM((1, LRU_WIDTH), jnp.float32),
            pltpu.VMEM((ts, ATTN_WIDTH), jnp.bfloat16),
            pltpu.VMEM((BLOCK + ts, KV_SLAB_WIDTH), jnp.bfloat16),
            pltpu.VMEM((BLOCK + ts, KV_SLAB_WIDTH), jnp.bfloat16),
            pltpu.VMEM((ts, LRU_WIDTH), jnp.bfloat16),
            pltpu.VMEM((ts, ATTN_WIDTH), jnp.bfloat16),
            pltpu.VMEM((ts, D_MODEL), jnp.float32),
            pltpu.VMEM((ts, D_MODEL), jnp.float32),
            pltpu.VMEM((D_MODEL // LANES, ts, LANES), jnp.float32),
            pltpu.VMEM((ts, D_MODEL), jnp.bfloat16),
        ],
        compiler_params=pltpu.CompilerParams(
            dimension_semantics=("arbitrary", "arbitrary"), vmem_limit_bytes=VMEM_LIMIT_BYTES),
    )(sinks, h3d, cos_t, sin_t, pre_g, w_in_p, conv_w, conv_b, w_gate, b_a, b_x, lam,
      w_proj, post_g)


def _pad_cast(*ws):
    zeros = jnp.zeros((ws[0].shape[0], PAD_COLS), jnp.bfloat16)
    return jnp.concatenate([w.astype(jnp.bfloat16) for w in ws] + [zeros], axis=1)


def _gate_weights(w_a, w_x):
    per = MXU_DIM // LRU_BLOCK_W
    eye = jnp.eye(per, dtype=w_a.dtype)

    def expand(w):
        w4 = w.reshape(LRU_BLOCKS // per, per, LRU_BLOCK_W, LRU_BLOCK_W)
        return jnp.einsum("gncd,nm->gncmd", w4, eye).reshape(LRU_BLOCKS // per, MXU_DIM, MXU_DIM)

    return (0.5 * jnp.concatenate([expand(w_a), expand(w_x)], axis=2)).astype(jnp.bfloat16)


def _rope_tables(seq_len):
    inv_freq = ROPE_THETA ** (-np.arange(HALF, dtype=np.float64) / HALF)
    ang = np.arange(seq_len, dtype=np.float64)[:, None] * inv_freq[None, :]
    cos, sin = np.cos(ang), np.sin(ang)
    return (jnp.asarray(np.concatenate([cos, cos, cos, cos], axis=1), jnp.float32),
            jnp.asarray(np.concatenate([-sin, sin, -sin, sin], axis=1), jnp.float32))


def kernel(x, ffn1_pre_g, ffn1_w_gu, ffn1_w_down, ffn1_post_g, mix_pre_g, w_in, conv_w, conv_b, lru_w_a, lru_b_a, lru_w_x, lru_b_x, lru_lambda, attn_sinks, w_proj_lru, w_proj_attn, w_out, mix_post_g, ffn2_pre_g, ffn2_w_gu, ffn2_w_down, ffn2_post_g):
    bsz, s, d = x.shape
    depth = ffn1_pre_g.shape[0]
    cos_t, sin_t = _rope_tables(s)
    bf = jnp.bfloat16
    h = x
    for l in range(depth):
        h = _ffn(h.reshape(bsz * s, d), ffn1_pre_g[l][None], ffn1_w_gu[l], ffn1_w_down[l],
                 ffn1_post_g[l][None]).reshape(bsz, s, d)
        h = _mixer(h, attn_sinks[l], cos_t, sin_t, mix_pre_g[l][None], w_in[l].astype(bf),
                   conv_w[l], conv_b[l][None], _gate_weights(lru_w_a[l], lru_w_x[l]),
                   lru_b_a[l][None], lru_b_x[l][None], lru_lambda[l][None],
                   _pad_cast(w_proj_lru[l], w_proj_attn[l], w_out[l]),
                   mix_post_g[l][None])
        h = _ffn(h.reshape(bsz * s, d), ffn2_pre_g[l][None], ffn2_w_gu[l], ffn2_w_down[l],
                 ffn2_post_g[l][None]).reshape(bsz, s, d)
    return h
```

```python
import functools

import jax
import jax.numpy as jnp
import numpy as np
from jax import lax
from jax.experimental import pallas as pl
from jax.experimental.pallas import tpu as pltpu

D_MODEL = 1024
LRU_WIDTH = 1024
LRU_BLOCKS = 16
LRU_BLOCK_W = LRU_WIDTH // LRU_BLOCKS
CONV_WIDTH = 4
LRU_C = 8.0
HEAD_DIM = 64
HALF = HEAD_DIM // 2
N_Q_HEADS = 16
N_KV_HEADS = 4
ATTN_WIDTH = N_Q_HEADS * HEAD_DIM
KV_WIDTH = N_KV_HEADS * HEAD_DIM
WINDOW = 128
BLOCK = 128
ROPE_THETA = 10000.0
D_FF = 2816
MACARON_SCALE = 0.5
NORM_EPS = 1e-6
MASK_VALUE = -1e30

LANES = 128
SUBLANES = 8
MXU_DIM = 256
VMEM_LIMIT_BYTES = 56 * 1024 * 1024

FFN_TOKENS = 512
FFN_CHUNK = 256
DOWN_CAST_ROWS = 176
MIX_TOKENS = 512
SEG_LEN = SUBLANES

GROUP_W = 2 * MXU_DIM
GROUPS = LRU_WIDTH // GROUP_W
C_GATE = 0
C_X = C_GATE + LRU_WIDTH
C_Q = C_X + LRU_WIDTH
C_K = C_Q + ATTN_WIDTH
C_V = C_K + KV_WIDTH
C_GL = C_V + KV_WIDTH
C_GA = C_GL + D_MODEL
IN_WIDTH = C_GA + D_MODEL
LOG2E = 1.4426950408889634
P_LRU, P_ATTN, P_OUT = 0, D_MODEL, 2 * D_MODEL
KV_SLAB_WIDTH = N_KV_HEADS * 2 * LANES
SUM_LANE_LOW, SUM_LANE_HIGH = HEAD_DIM, 0
PAD_COLS = LANES


def _rms_norm(x, g):
    return x * lax.rsqrt(jnp.mean(x * x, axis=-1, keepdims=True) + NORM_EPS) * g


def _sigmoid(x):
    return 0.5 * jnp.tanh(0.5 * x) + 0.5


def _const_spec(shape):
    n = len(shape)
    return pl.BlockSpec(shape, lambda *_: (0,) * n, pipeline_mode=pl.Buffered(1))


def _ffn_kernel(*refs, emit_next_norm):
    if emit_next_norm:
        (h_ref, pre_g_ref, w_gu_hbm, w_down_hbm, post_g_ref, next_g_ref, o_ref, un_ref, unp_ref,
         w_gu_ref, w_down_ref, stage_gu_ref, stage_down_ref, sem_gu_ref, sem_down_ref, act_ref,
         unf_ref) = refs
    else:
        (h_ref, pre_g_ref, w_gu_hbm, w_down_hbm, post_g_ref, o_ref,
         w_gu_ref, w_down_ref, stage_gu_ref, stage_down_ref, sem_gu_ref, sem_down_ref, act_ref) = refs
    bf16, f32 = jnp.bfloat16, jnp.float32
    n_chunks = D_FF // FFN_CHUNK

    def gu_copies(c):
        slot = c % 2
        return [pltpu.make_async_copy(w_gu_hbm.at[:, pl.ds(part * D_FF + c * FFN_CHUNK, FFN_CHUNK)],
                                      stage_gu_ref.at[slot, part], sem_gu_ref.at[2 * slot + part])
                for part in range(2)]

    down_copy = pltpu.make_async_copy(w_down_hbm, stage_down_ref, sem_down_ref.at[0])

    def body(stage_weights):
        if stage_weights:
            down_copy.start()
            for cp in gu_copies(0):
                cp.start()
        xn = _rms_norm(h_ref[...], pre_g_ref[...]).astype(bf16)
        for c in range(n_chunks):
            lo = c * FFN_CHUNK
            if stage_weights:
                if c + 1 < n_chunks:
                    for cp in gu_copies(c + 1):
                        cp.start()
                for part, cp in enumerate(gu_copies(c)):
                    cp.wait()
                    col = part * D_FF + lo
                    w_gu_ref[:, col:col + FFN_CHUNK] = stage_gu_ref[c % 2, part].astype(bf16)
            g = jnp.dot(xn, w_gu_ref[:, lo:lo + FFN_CHUNK], preferred_element_type=f32)
            u = jnp.dot(xn, w_gu_ref[:, D_FF + lo:D_FF + lo + FFN_CHUNK], preferred_element_type=f32)
            act_ref[:, lo:lo + FFN_CHUNK] = (jax.nn.silu(g) * u).astype(bf16)
        if stage_weights:
            down_copy.wait()
            for r in range(0, D_FF, DOWN_CAST_ROWS):
                w_down_ref[r:r + DOWN_CAST_ROWS, 0:D_MODEL] = (
                    stage_down_ref[r:r + DOWN_CAST_ROWS, :].astype(bf16))
        f = [jnp.dot(act_ref[...], w_down_ref[:, n * GROUP_W:(n + 1) * GROUP_W],
                     preferred_element_type=f32) for n in range(D_MODEL // GROUP_W)]
        mean_sq = sum(jnp.sum(fn * fn, axis=1, keepdims=True) for fn in f) * (1.0 / D_MODEL)
        scale = MACARON_SCALE * lax.rsqrt(mean_sq + NORM_EPS)
        out = []
        for n in range(D_MODEL // GROUP_W):
            cn = slice(n * GROUP_W, (n + 1) * GROUP_W)
            out.append(h_ref[:, cn] + f[n] * scale * post_g_ref[:, cn])
            o_ref[:, cn] = out[n]
        if emit_next_norm:
            seg_rows = FFN_TOKENS // SEG_LEN
            mean_sq = sum(jnp.sum(on * on, axis=1, keepdims=True) for on in out) * (1.0 / D_MODEL)
            inv_rms = lax.rsqrt(mean_sq + NORM_EPS)
            for n in range(D_MODEL // GROUP_W):
                cn = slice(n * GROUP_W, (n + 1) * GROUP_W)
                un = out[n] * inv_rms * next_g_ref[:, cn]
                un_ref[:, cn] = un.astype(bf16)
                for k in range(GROUP_W // LANES):
                    unf_ref[n * (GROUP_W // LANES) + k] = un[:, k * LANES:(k + 1) * LANES]
            for lg in range(D_MODEL // LANES):
                pieces = [unf_ref.at[lg][pl.ds(SEG_LEN * SUBLANES * grp + i, SUBLANES, stride=SEG_LEN), :]
                          for i in range(SEG_LEN) for grp in range(seg_rows // SUBLANES)]
                unp_ref[:, lg * LANES:(lg + 1) * LANES] = jnp.concatenate(pieces, axis=0).astype(bf16)

    first = pl.program_id(0) == 0
    pl.when(first)(lambda: body(True))
    pl.when(jnp.logical_not(first))(lambda: body(False))


def _ffn(h2d, pre_g, w_gu, w_down, post_g, next_g=None):
    m = h2d.shape[0]
    emit = next_g is not None
    assert not emit or FFN_TOKENS == MIX_TOKENS, "the permuted copy is laid out per mixer tile"
    tile = pl.BlockSpec((FFN_TOKENS, D_MODEL), lambda i: (i, 0))
    out_f32 = jax.ShapeDtypeStruct((m, D_MODEL), jnp.float32)
    out_bf16 = jax.ShapeDtypeStruct((m, D_MODEL), jnp.bfloat16)
    return pl.pallas_call(
        functools.partial(_ffn_kernel, emit_next_norm=emit),
        name="ffn",
        grid=(m // FFN_TOKENS,),
        in_specs=[
            tile,
            _const_spec((1, D_MODEL)),
            pl.BlockSpec(memory_space=pl.ANY),
            pl.BlockSpec(memory_space=pl.ANY),
            _const_spec((1, D_MODEL)),
        ] + ([_const_spec((1, D_MODEL))] if emit else []),
        out_specs=(tile, tile, tile) if emit else tile,
        out_shape=(out_f32, out_bf16, out_bf16) if emit else out_f32,
        scratch_shapes=[
            pltpu.VMEM((D_MODEL, 2 * D_FF), jnp.bfloat16),
            pltpu.VMEM((D_FF, D_MODEL + PAD_COLS), jnp.bfloat16),
            pltpu.VMEM((2, 2, D_MODEL, FFN_CHUNK), jnp.float32),
            pltpu.VMEM((D_FF, D_MODEL), jnp.float32),
            pltpu.SemaphoreType.DMA((4,)),
            pltpu.SemaphoreType.DMA((1,)),
            pltpu.VMEM((FFN_TOKENS, D_FF), jnp.bfloat16),
        ] + ([pltpu.VMEM((D_MODEL // LANES, FFN_TOKENS, LANES), jnp.float32)] if emit else []),
        compiler_params=pltpu.CompilerParams(
            dimension_semantics=("arbitrary",), vmem_limit_bytes=VMEM_LIMIT_BYTES),
    )(*((h2d, pre_g, w_gu, w_down, post_g) + ((next_g,) if emit else ())))


def _mixer_kernel(sinks_ref, h_ref, un_ref, unp_ref, cos_ref, sin_ref, w_in_ref, conv_w_ref,
                  conv_b_ref, w_gate_ref, b_a_ref, b_x_ref, lam_ref, w_proj_ref, post_g_ref, o_ref,
                  xhist_ref, hstate_ref, q_ref, kall_ref, vall_ref, ylru_ref,
                  yattn_ref, sgl_ref, sga_ref, plru_ref, merged_ref):
    ts = MIX_TOKENS
    seg_rows = ts // SEG_LEN
    t_idx = pl.program_id(1)
    f32, bf16 = jnp.float32, jnp.bfloat16

    @pl.when(t_idx == 0)
    def _():
        xhist_ref[...] = jnp.zeros_like(xhist_ref)
        hstate_ref[...] = jnp.zeros_like(hstate_ref)
        kall_ref[0:BLOCK, :] = jnp.zeros((BLOCK, KV_SLAB_WIDTH), bf16)
        vall_ref[0:BLOCK, :] = jnp.zeros((BLOCK, KV_SLAB_WIDTH), bf16)

    def proj(lo, src_ref=un_ref):
        return jnp.dot(src_ref[...], w_in_ref[:, lo:lo + GROUP_W], preferred_element_type=f32)

    lane = lax.broadcasted_iota(jnp.int32, (ts, LANES), 1)
    first_half = (lane % HEAD_DIM) < HALF
    first_head = lane < HEAD_DIM

    def rope(x, c, s):
        partner = jnp.where(first_half, pltpu.roll(x, LANES - HALF, axis=1),
                            pltpu.roll(x, HALF, axis=1))
        return x * c + partner * s

    cos = cos_ref[...]
    sin = sin_ref[...]
    q_scale = (HEAD_DIM ** -0.5) * LOG2E
    cos_q = cos * q_scale
    sin_q = sin * q_scale
    seg_row = lax.broadcasted_iota(jnp.int32, (seg_rows, GROUP_W), 0)

    def shift_rows(cur, first_row):
        return jnp.where(seg_row == 0, first_row, pltpu.roll(cur, 1, axis=0))

    kv_nat = proj(C_K)
    k_nat, v_nat = kv_nat[:, :KV_WIDTH], kv_nat[:, KV_WIDTH:]
    for jj in range(N_KV_HEADS // 2):
        kr = rope(k_nat[:, jj * LANES:(jj + 1) * LANES], cos, sin)
        vv = v_nat[:, jj * LANES:(jj + 1) * LANES]
        kr_swapped = pltpu.roll(kr, HEAD_DIM, axis=1)
        vv_swapped = pltpu.roll(vv, HEAD_DIM, axis=1)
        for half in range(2):
            j = 2 * jj + half
            for dst_ref, same, swapped, fill_low, fill_high in (
                    (kall_ref, kr, kr_swapped, 0.0, 0.0),
                    (vall_ref, vv, vv_swapped, (lane == SUM_LANE_LOW).astype(f32),
                     (lane == SUM_LANE_HIGH).astype(f32))):
                low, high = (same, swapped) if half == 0 else (swapped, same)
                dst_ref[BLOCK:BLOCK + ts, (2 * j) * LANES:(2 * j + 1) * LANES] = (
                    jnp.where(first_head, low, fill_low).astype(bf16))
                dst_ref[BLOCK:BLOCK + ts, (2 * j + 1) * LANES:(2 * j + 2) * LANES] = (
                    jnp.where(first_head, fill_high, high).astype(bf16))

    for g in range(GROUPS):
        qg = proj(C_Q + g * GROUP_W)
        for pp in range(GROUP_W // LANES):
            lo = g * GROUP_W + pp * LANES
            q_ref[:, lo:lo + LANES] = rope(qg[:, pp * LANES:(pp + 1) * LANES], cos_q, sin_q).astype(bf16)

    def lru_stages(g):
        cg = slice(g * GROUP_W, (g + 1) * GROUP_W)
        st = {}

        def conv_and_gates():
            xg = proj(C_X + g * GROUP_W, unp_ref)
            x_blk = [xg[i * seg_rows:(i + 1) * seg_rows, :] for i in range(SEG_LEN)]
            wrapped = {}
            for i_src in range(SEG_LEN - CONV_WIDTH + 1, SEG_LEN):
                k = i_src - (SEG_LEN - CONV_WIDTH + 1)
                wrapped[i_src] = shift_rows(x_blk[i_src], xhist_ref[k, :, cg])
                xhist_ref[k, :, cg] = x_blk[i_src][seg_rows - 1:seg_rows, :]
            xc_blk = []
            for i in range(SEG_LEN):
                acc = conv_b_ref[:, cg]
                for d in range(CONV_WIDTH):
                    src = x_blk[i - d] if i >= d else wrapped[i - d + SEG_LEN]
                    acc = acc + src * conv_w_ref[CONV_WIDTH - 1 - d:CONV_WIDTH - d, cg]
                xc_blk.append(acc)
            st["xc"] = jnp.concatenate(xc_blk, axis=0)
            xc_b = st["xc"].astype(bf16)
            st["gates"] = [
                jnp.dot(xc_b[:, k * MXU_DIM:(k + 1) * MXU_DIM], w_gate_ref[g * (GROUP_W // MXU_DIM) + k],
                        preferred_element_type=f32) for k in range(GROUP_W // MXU_DIM)]

        def decay_and_input():
            gates, xc = st["gates"], st["xc"]
            th_r = jnp.tanh(jnp.concatenate([ri[:, :MXU_DIM] for ri in gates], axis=1)
                            + 0.5 * b_a_ref[:, cg])
            th_i = jnp.tanh(jnp.concatenate([ri[:, MXU_DIM:] for ri in gates], axis=1)
                            + 0.5 * b_x_ref[:, cg])
            lam = lam_ref[:, cg]
            softplus_neg_lam = jnp.maximum(-lam, 0.0) + jnp.log1p(jnp.exp(-jnp.abs(lam)))
            k = (0.5 * LRU_C) * softplus_neg_lam
            neg_log_a = th_r * k + k
            a = jnp.exp2(neg_log_a * (-LOG2E))
            t = jnp.tanh(neg_log_a) * (a * a + 1.0)
            mult = jnp.where(t > 0.0, t * lax.rsqrt(t), 0.0)
            half_xc = 0.5 * xc
            st["a"], st["b"] = a, mult * (th_i * half_xc + half_xc)

        def local_scan():
            a, b = st["a"], st["b"]
            a_blk = [a[i * seg_rows:(i + 1) * seg_rows, :] for i in range(SEG_LEN)]
            b_blk = [b[i * seg_rows:(i + 1) * seg_rows, :] for i in range(SEG_LEN)]
            h_loc, a_cum = [b_blk[0]], [a_blk[0]]
            for i in range(1, SEG_LEN):
                h_loc.append(a_blk[i] * h_loc[-1] + b_blk[i])
                a_cum.append(a_blk[i] * a_cum[-1])
            st["h_loc"], st["a_cum"] = h_loc, a_cum

        def carry_and_gate():
            h_loc, a_cum = st["h_loc"], st["a_cum"]
            seg_a, seg_h = a_cum[-1], h_loc[-1]
            shift = 1
            while shift < SUBLANES:
                keep = (seg_row % SUBLANES) >= shift
                a_prev = jnp.where(keep, pltpu.roll(seg_a, shift, axis=0), 1.0)
                h_prev = jnp.where(keep, pltpu.roll(seg_h, shift, axis=0), 0.0)
                seg_h = seg_a * h_prev + seg_h
                seg_a = seg_a * a_prev
                shift *= 2
            carry = hstate_ref[:, cg]
            state_in = carry
            ends = []
            for v in range(seg_rows // SUBLANES):
                lo = v * SUBLANES
                hv = seg_a[lo:lo + SUBLANES, :] * carry + seg_h[lo:lo + SUBLANES, :]
                ends.append(hv)
                carry = hv[SUBLANES - 1:SUBLANES, :]
            hstate_ref[:, cg] = carry
            seg_in = shift_rows(jnp.concatenate(ends, axis=0), state_in)
            h_lru = jnp.concatenate([h_loc[i] + a_cum[i] * seg_in for i in range(SEG_LEN)], axis=0)
            ylru_ref[:, cg] = (h_lru * jax.nn.gelu(proj(C_GATE + g * GROUP_W, unp_ref))).astype(bf16)

        def merge_gates():
            sgl_ref[:, cg] = _sigmoid(proj(C_GL + g * GROUP_W))
            sga_ref[:, cg] = _sigmoid(proj(C_GA + g * GROUP_W))

        return [conv_and_gates, decay_and_input, local_scan, carry_and_gate, merge_gates]

    first_head2 = lax.broadcasted_iota(jnp.int32, (2 * BLOCK, LANES), 1) < HEAD_DIM
    qi = lax.broadcasted_iota(jnp.int32, (2 * BLOCK, 2 * BLOCK), 0) % BLOCK
    si = lax.broadcasted_iota(jnp.int32, (2 * BLOCK, 2 * BLOCK), 1)
    diff = BLOCK + qi - si
    band = (diff >= 0) & (diff < WINDOW)
    first_pair_rows = lax.broadcasted_iota(jnp.int32, (2 * BLOCK, 1), 0) < BLOCK

    def scores(j, qb):
        r0 = qb * BLOCK
        q2 = jnp.concatenate(
            [q_ref[r0:r0 + BLOCK, (2 * j + pp) * LANES:(2 * j + pp + 1) * LANES] for pp in range(2)],
            axis=0)
        mask = band & ((si >= BLOCK) | (t_idx * (ts // BLOCK) + qb > 0))
        halves = []
        for second in range(2):
            slab = slice((2 * j + second) * LANES, (2 * j + second + 1) * LANES)
            head = 4 * j + second
            sink = jnp.where(first_pair_rows, sinks_ref[head] * LOG2E, sinks_ref[head + 2] * LOG2E)
            s = lax.dot_general(q2, kall_ref[r0:r0 + 2 * BLOCK, slab], (((1,), (1,)), ((), ())),
                                preferred_element_type=f32)
            s = jnp.where(mask, s, MASK_VALUE)
            m = jnp.maximum(jnp.max(s, axis=1, keepdims=True), sink)
            halves.append((s, m, sink))
        return halves

    def weighted_values(j, qb, halves):
        r0 = qb * BLOCK
        parts = []
        for second, (s, m, sink) in enumerate(halves):
            slab = slice((2 * j + second) * LANES, (2 * j + second + 1) * LANES)
            e = jnp.exp2(s - m).astype(bf16)
            pv = jnp.dot(e, vall_ref[r0:r0 + 2 * BLOCK, slab], preferred_element_type=f32)
            sum_lane = SUM_LANE_HIGH if second else SUM_LANE_LOW
            denom = pv[:, sum_lane:sum_lane + 1] + jnp.exp2(sink - m)
            parts.append(pv * (1.0 / denom))
        out = jnp.where(first_head2, parts[0], parts[1]).astype(bf16)
        for pp in range(2):
            lo = (2 * j + pp) * LANES
            yattn_ref[r0:r0 + BLOCK, lo:lo + LANES] = out[pp * BLOCK:(pp + 1) * BLOCK, :]

    def attention_stages():
        blocks = [(j, qb) for j in range(N_KV_HEADS) for qb in range(ts // BLOCK)]
        pending = {}

        def stage(n):
            def run():
                if n < len(blocks):
                    pending[n] = scores(*blocks[n])
                if n >= 1:
                    weighted_values(*blocks[n - 1], pending.pop(n - 1))
            return run

        return [stage(n) for n in range(len(blocks) + 1)]

    def lru_projection(c):
        p_lru = jnp.dot(ylru_ref[...], w_proj_ref[:, P_LRU + c * GROUP_W:P_LRU + (c + 1) * GROUP_W],
                        preferred_element_type=f32)
        for half in range(GROUP_W // LANES):
            lg = c * (GROUP_W // LANES) + half
            for i in range(SEG_LEN):
                for grp in range(seg_rows // SUBLANES):
                    row = i * seg_rows + grp * SUBLANES
                    plru_ref.at[lg][pl.ds(SEG_LEN * SUBLANES * grp + i, SUBLANES, stride=SEG_LEN), :] = (
                        p_lru[row:row + SUBLANES, half * LANES:(half + 1) * LANES])

    lru_list = [stage for g in range(GROUPS) for stage in lru_stages(g)]
    lru_list += [lambda c=c: lru_projection(c) for c in range(GROUPS)]
    att_list = attention_stages()
    att_per_lru = -(-len(att_list) // len(lru_list))
    while lru_list or att_list:
        if lru_list:
            lru_list.pop(0)()
        for _ in range(att_per_lru):
            if att_list:
                att_list.pop(0)()

    kall_ref[0:BLOCK, :] = kall_ref[ts:ts + BLOCK, :]
    vall_ref[0:BLOCK, :] = vall_ref[ts:ts + BLOCK, :]

    for n in range(GROUPS):
        cn = slice(n * GROUP_W, (n + 1) * GROUP_W)
        p_attn = jnp.dot(yattn_ref[...], w_proj_ref[:, P_ATTN + n * GROUP_W:P_ATTN + (n + 1) * GROUP_W],
                         preferred_element_type=f32)
        p_lru = jnp.concatenate([plru_ref[n * (GROUP_W // LANES) + half] for half in range(GROUP_W // LANES)],
                                axis=1)
        merged_ref[:, cn] = (sgl_ref[:, cn] * p_lru + sga_ref[:, cn] * p_attn).astype(bf16)
    m_out = [jnp.dot(merged_ref[...], w_proj_ref[:, P_OUT + n * GROUP_W:P_OUT + (n + 1) * GROUP_W],
                     preferred_element_type=f32) for n in range(GROUPS)]
    mean_sq = sum(jnp.sum(mo * mo, axis=1, keepdims=True) for mo in m_out) * (1.0 / D_MODEL)
    inv_rms = lax.rsqrt(mean_sq + NORM_EPS)
    for n in range(GROUPS):
        cn = slice(n * GROUP_W, (n + 1) * GROUP_W)
        o_ref[:, cn] = h_ref[:, cn] + m_out[n] * inv_rms * post_g_ref[:, cn]


def _mixer(h3d, un3d, unp3d, sinks, cos_t, sin_t, w_in_p, conv_w, conv_b, w_gate, b_a, b_x, lam,
           w_proj, post_g):
    bsz, s, _ = h3d.shape
    ts = MIX_TOKENS
    tile = pl.BlockSpec((None, ts, D_MODEL), lambda b, t: (b, t, 0))
    rope_spec = pl.BlockSpec((ts, LANES), lambda b, t: (t, 0))
    return pl.pallas_call(
        _mixer_kernel,
        name="mixer",
        grid=(bsz, s // ts),
        in_specs=[
            pl.BlockSpec(memory_space=pltpu.SMEM),
            tile, tile, tile, rope_spec, rope_spec,
            _const_spec((D_MODEL, IN_WIDTH)),
            _const_spec((CONV_WIDTH, LRU_WIDTH)),
            _const_spec((1, LRU_WIDTH)),
            _const_spec((LRU_WIDTH // MXU_DIM, MXU_DIM, 2 * MXU_DIM)),
            _const_spec((1, LRU_WIDTH)),
            _const_spec((1, LRU_WIDTH)),
            _const_spec((1, LRU_WIDTH)),
            _const_spec((D_MODEL, 3 * D_MODEL + PAD_COLS)),
            _const_spec((1, D_MODEL)),
        ],
        out_specs=tile,
        out_shape=jax.ShapeDtypeStruct(h3d.shape, jnp.float32),
        scratch_shapes=[
            pltpu.VMEM((CONV_WIDTH - 1, 1, LRU_WIDTH), jnp.float32),
            pltpu.VMEM((1, LRU_WIDTH), jnp.float32),
            pltpu.VMEM((ts, ATTN_WIDTH), jnp.bfloat16),
            pltpu.VMEM((BLOCK + ts, KV_SLAB_WIDTH), jnp.bfloat16),
            pltpu.VMEM((BLOCK + ts, KV_SLAB_WIDTH), jnp.bfloat16),
            pltpu.VMEM((ts, LRU_WIDTH), jnp.bfloat16),
            pltpu.VMEM((ts, ATTN_WIDTH), jnp.bfloat16),
            pltpu.VMEM((ts, D_MODEL), jnp.float32),
            pltpu.VMEM((ts, D_MODEL), jnp.float32),
            pltpu.VMEM((D_MODEL // LANES, ts, LANES), jnp.float32),
            pltpu.VMEM((ts, D_MODEL), jnp.bfloat16),
        ],
        compiler_params=pltpu.CompilerParams(
            dimension_semantics=("arbitrary", "arbitrary"), vmem_limit_bytes=VMEM_LIMIT_BYTES),
    )(sinks, h3d, un3d, unp3d, cos_t, sin_t, w_in_p, conv_w, conv_b, w_gate, b_a, b_x, lam,
      w_proj, post_g)


def _pad_cast(*ws):
    zeros = jnp.zeros((ws[0].shape[0], PAD_COLS), jnp.bfloat16)
    return jnp.concatenate([w.astype(jnp.bfloat16) for w in ws] + [zeros], axis=1)


def _gate_weights(w_a, w_x):
    per = MXU_DIM // LRU_BLOCK_W
    eye = jnp.eye(per, dtype=w_a.dtype)

    def expand(w):
        w4 = w.reshape(LRU_BLOCKS // per, per, LRU_BLOCK_W, LRU_BLOCK_W)
        return jnp.einsum("gncd,nm->gncmd", w4, eye).reshape(LRU_BLOCKS // per, MXU_DIM, MXU_DIM)

    return (0.5 * jnp.concatenate([expand(w_a), expand(w_x)], axis=2)).astype(jnp.bfloat16)


def _rope_tables(seq_len):
    inv_freq = ROPE_THETA ** (-np.arange(HALF, dtype=np.float64) / HALF)
    ang = np.arange(seq_len, dtype=np.float64)[:, None] * inv_freq[None, :]
    cos, sin = np.cos(ang), np.sin(ang)
    return (jnp.asarray(np.concatenate([cos, cos, cos, cos], axis=1), jnp.float32),
            jnp.asarray(np.concatenate([-sin, sin, -sin, sin], axis=1), jnp.float32))


def kernel(x, ffn1_pre_g, ffn1_w_gu, ffn1_w_down, ffn1_post_g, mix_pre_g, w_in, conv_w, conv_b, lru_w_a, lru_b_a, lru_w_x, lru_b_x, lru_lambda, attn_sinks, w_proj_lru, w_proj_attn, w_out, mix_post_g, ffn2_pre_g, ffn2_w_gu, ffn2_w_down, ffn2_post_g):
    bsz, s, d = x.shape
    depth = ffn1_pre_g.shape[0]
    cos_t, sin_t = _rope_tables(s)
    bf = jnp.bfloat16
    h = x
    for l in range(depth):
        h, un, unp = (a.reshape(bsz, s, d) for a in _ffn(
            h.reshape(bsz * s, d), ffn1_pre_g[l][None], ffn1_w_gu[l], ffn1_w_down[l],
            ffn1_post_g[l][None], next_g=mix_pre_g[l][None]))
        h = _mixer(h, un, unp, attn_sinks[l], cos_t, sin_t, w_in[l].astype(bf),
                   conv_w[l], conv_b[l][None], _gate_weights(lru_w_a[l], lru_w_x[l]),
                   lru_b_a[l][None], lru_b_x[l][None], lru_lambda[l][None],
                   _pad_cast(w_proj_lru[l], w_proj_attn[l], w_out[l]),
                   mix_post_g[l][None])
        h = _ffn(h.reshape(bsz * s, d), ffn2_pre_g[l][None], ffn2_w_gu[l], ffn2_w_down[l],
                 ffn2_post_g[l][None]).reshape(bsz, s, d)
    return h
```

```python
import jax
import jax.numpy as jnp
import numpy as np
from jax import lax
from jax.experimental import pallas as pl
from jax.experimental.pallas import tpu as pltpu

D_MODEL = 1024
LRU_WIDTH = 1024
LRU_BLOCKS = 16
LRU_BLOCK_W = LRU_WIDTH // LRU_BLOCKS
CONV_WIDTH = 4
LRU_C = 8.0
HEAD_DIM = 64
HALF = HEAD_DIM // 2
N_Q_HEADS = 16
N_KV_HEADS = 4
ATTN_WIDTH = N_Q_HEADS * HEAD_DIM
KV_WIDTH = N_KV_HEADS * HEAD_DIM
WINDOW = 128
BLOCK = 128
ROPE_THETA = 10000.0
D_FF = 2816
MACARON_SCALE = 0.5
NORM_EPS = 1e-6
MASK_VALUE = -1e30

LANES = 128
SUBLANES = 8
MXU_DIM = 256
VMEM_LIMIT_BYTES = 56 * 1024 * 1024

FFN_TOKENS = 512
FFN_CHUNK = 256
DOWN_CAST_ROWS = 176
MIX_TOKENS = 512
SEG_LEN = SUBLANES

GROUP_W = 2 * MXU_DIM
GROUPS = LRU_WIDTH // GROUP_W
C_GATE = 0
C_X = C_GATE + LRU_WIDTH
C_Q = C_X + LRU_WIDTH
C_K = C_Q + ATTN_WIDTH
C_V = C_K + KV_WIDTH
C_GL = C_V + KV_WIDTH
C_GA = C_GL + D_MODEL
IN_WIDTH = C_GA + D_MODEL
LOG2E = 1.4426950408889634
P_LRU, P_ATTN, P_OUT = 0, D_MODEL, 2 * D_MODEL
KV_SLAB_WIDTH = N_KV_HEADS * 2 * LANES
SUM_LANE_LOW, SUM_LANE_HIGH = HEAD_DIM, 0
PAD_COLS = LANES


def _rms_norm(x, g):
    return x * lax.rsqrt(jnp.mean(x * x, axis=-1, keepdims=True) + NORM_EPS) * g


def _sigmoid(x):
    return 0.5 * jnp.tanh(0.5 * x) + 0.5


def _const_spec(shape):
    n = len(shape)
    return pl.BlockSpec(shape, lambda *_: (0,) * n, pipeline_mode=pl.Buffered(1))


def _ffn_kernel(h_ref, pre_g_ref, w_gu_hbm, w_down_hbm, post_g_ref, o_ref,
                w_gu_ref, w_down_ref, stage_gu_ref, stage_down_ref, sem_gu_ref, sem_down_ref, act_ref):
    bf16, f32 = jnp.bfloat16, jnp.float32
    n_chunks = D_FF // FFN_CHUNK

    def gu_copies(c):
        slot = c % 2
        return [pltpu.make_async_copy(w_gu_hbm.at[:, pl.ds(part * D_FF + c * FFN_CHUNK, FFN_CHUNK)],
                                      stage_gu_ref.at[slot, part], sem_gu_ref.at[2 * slot + part])
                for part in range(2)]

    down_copy = pltpu.make_async_copy(w_down_hbm, stage_down_ref, sem_down_ref.at[0])

    def body(stage_weights):
        if stage_weights:
            down_copy.start()
            for cp in gu_copies(0):
                cp.start()
        xn = _rms_norm(h_ref[...], pre_g_ref[...]).astype(bf16)
        for c in range(n_chunks):
            lo = c * FFN_CHUNK
            if stage_weights:
                if c + 1 < n_chunks:
                    for cp in gu_copies(c + 1):
                        cp.start()
                for part, cp in enumerate(gu_copies(c)):
                    cp.wait()
                    col = part * D_FF + lo
                    w_gu_ref[:, col:col + FFN_CHUNK] = stage_gu_ref[c % 2, part].astype(bf16)
            g = jnp.dot(xn, w_gu_ref[:, lo:lo + FFN_CHUNK], preferred_element_type=f32)
            u = jnp.dot(xn, w_gu_ref[:, D_FF + lo:D_FF + lo + FFN_CHUNK], preferred_element_type=f32)
            act_ref[:, lo:lo + FFN_CHUNK] = (jax.nn.silu(g) * u).astype(bf16)
        if stage_weights:
            down_copy.wait()
            for r in range(0, D_FF, DOWN_CAST_ROWS):
                w_down_ref[r:r + DOWN_CAST_ROWS, 0:D_MODEL] = (
                    stage_down_ref[r:r + DOWN_CAST_ROWS, :].astype(bf16))
        f = [jnp.dot(act_ref[...], w_down_ref[:, n * GROUP_W:(n + 1) * GROUP_W],
                     preferred_element_type=f32) for n in range(D_MODEL // GROUP_W)]
        mean_sq = sum(jnp.sum(fn * fn, axis=1, keepdims=True) for fn in f) * (1.0 / D_MODEL)
        scale = MACARON_SCALE * lax.rsqrt(mean_sq + NORM_EPS)
        for n in range(D_MODEL // GROUP_W):
            cn = slice(n * GROUP_W, (n + 1) * GROUP_W)
            o_ref[:, cn] = h_ref[:, cn] + f[n] * scale * post_g_ref[:, cn]

    first = pl.program_id(0) == 0
    pl.when(first)(lambda: body(True))
    pl.when(jnp.logical_not(first))(lambda: body(False))


def _ffn(h2d, pre_g, w_gu, w_down, post_g):
    m = h2d.shape[0]
    tile = pl.BlockSpec((FFN_TOKENS, D_MODEL), lambda i: (i, 0))
    return pl.pallas_call(
        _ffn_kernel,
        name="ffn",
        grid=(m // FFN_TOKENS,),
        in_specs=[
            tile,
            _const_spec((1, D_MODEL)),
            pl.BlockSpec(memory_space=pl.ANY),
            pl.BlockSpec(memory_space=pl.ANY),
            _const_spec((1, D_MODEL)),
        ],
        out_specs=tile,
        out_shape=jax.ShapeDtypeStruct((m, D_MODEL), jnp.float32),
        scratch_shapes=[
            pltpu.VMEM((D_MODEL, 2 * D_FF), jnp.bfloat16),
            pltpu.VMEM((D_FF, D_MODEL + PAD_COLS), jnp.bfloat16),
            pltpu.VMEM((2, 2, D_MODEL, FFN_CHUNK), jnp.float32),
            pltpu.VMEM((D_FF, D_MODEL), jnp.float32),
            pltpu.SemaphoreType.DMA((4,)),
            pltpu.SemaphoreType.DMA((1,)),
            pltpu.VMEM((FFN_TOKENS, D_FF), jnp.bfloat16),
        ],
        compiler_params=pltpu.CompilerParams(
            dimension_semantics=("arbitrary",), vmem_limit_bytes=VMEM_LIMIT_BYTES),
    )(h2d, pre_g, w_gu, w_down, post_g)


def _mixer_kernel(sinks_ref, h_ref, *rest):
    hp_refs, rest = rest[:SEG_LEN], rest[SEG_LEN:]
    _mixer_body(sinks_ref, h_ref, hp_refs, *rest)


def _mixer_body(sinks_ref, h_ref, hp_refs, cos_ref, sin_ref, pre_g_ref, w_in_ref, conv_w_ref,
                conv_b_ref, w_gate_ref, b_a_ref, b_x_ref, lam_ref, w_proj_ref, post_g_ref, o_ref,
                un_ref, unp_ref, xhist_ref, hstate_ref, q_ref, kall_ref, vall_ref, ylru_ref,
                  yattn_ref, sgl_ref, sga_ref, plru_ref, merged_ref):
    ts = MIX_TOKENS
    seg_rows = ts // SEG_LEN
    t_idx = pl.program_id(1)
    f32, bf16 = jnp.float32, jnp.bfloat16

    @pl.when(t_idx == 0)
    def _():
        xhist_ref[...] = jnp.zeros_like(xhist_ref)
        hstate_ref[...] = jnp.zeros_like(hstate_ref)
        kall_ref[0:BLOCK, :] = jnp.zeros((BLOCK, KV_SLAB_WIDTH), bf16)
        vall_ref[0:BLOCK, :] = jnp.zeros((BLOCK, KV_SLAB_WIDTH), bf16)

    un_ref[...] = _rms_norm(h_ref[...], pre_g_ref[...]).astype(bf16)
    for i in range(SEG_LEN):
        unp_ref[i * seg_rows:(i + 1) * seg_rows, :] = _rms_norm(hp_refs[i][...], pre_g_ref[...]).astype(bf16)

    def proj(lo, src_ref=un_ref):
        return jnp.dot(src_ref[...], w_in_ref[:, lo:lo + GROUP_W], preferred_element_type=f32)

    lane = lax.broadcasted_iota(jnp.int32, (ts, LANES), 1)
    first_half = (lane % HEAD_DIM) < HALF
    first_head = lane < HEAD_DIM

    def rope(x, c, s):
        partner = jnp.where(first_half, pltpu.roll(x, LANES - HALF, axis=1),
                            pltpu.roll(x, HALF, axis=1))
        return x * c + partner * s

    cos = cos_ref[...]
    sin = sin_ref[...]
    q_scale = (HEAD_DIM ** -0.5) * LOG2E
    cos_q = cos * q_scale
    sin_q = sin * q_scale
    seg_row = lax.broadcasted_iota(jnp.int32, (seg_rows, GROUP_W), 0)

    def shift_rows(cur, first_row):
        return jnp.where(seg_row == 0, first_row, pltpu.roll(cur, 1, axis=0))

    kv_nat = proj(C_K)
    k_nat, v_nat = kv_nat[:, :KV_WIDTH], kv_nat[:, KV_WIDTH:]
    for jj in range(N_KV_HEADS // 2):
        kr = rope(k_nat[:, jj * LANES:(jj + 1) * LANES], cos, sin)
        vv = v_nat[:, jj * LANES:(jj + 1) * LANES]
        kr_swapped = pltpu.roll(kr, HEAD_DIM, axis=1)
        vv_swapped = pltpu.roll(vv, HEAD_DIM, axis=1)
        for half in range(2):
            j = 2 * jj + half
            for dst_ref, same, swapped, fill_low, fill_high in (
                    (kall_ref, kr, kr_swapped, 0.0, 0.0),
                    (vall_ref, vv, vv_swapped, (lane == SUM_LANE_LOW).astype(f32),
                     (lane == SUM_LANE_HIGH).astype(f32))):
                low, high = (same, swapped) if half == 0 else (swapped, same)
                dst_ref[BLOCK:BLOCK + ts, (2 * j) * LANES:(2 * j + 1) * LANES] = (
                    jnp.where(first_head, low, fill_low).astype(bf16))
                dst_ref[BLOCK:BLOCK + ts, (2 * j + 1) * LANES:(2 * j + 2) * LANES] = (
                    jnp.where(first_head, fill_high, high).astype(bf16))

    for g in range(GROUPS):
        qg = proj(C_Q + g * GROUP_W)
        for pp in range(GROUP_W // LANES):
            lo = g * GROUP_W + pp * LANES
            q_ref[:, lo:lo + LANES] = rope(qg[:, pp * LANES:(pp + 1) * LANES], cos_q, sin_q).astype(bf16)

    def lru_stages(g):
        cg = slice(g * GROUP_W, (g + 1) * GROUP_W)
        st = {}

        def conv_and_gates():
            xg = proj(C_X + g * GROUP_W, unp_ref)
            x_blk = [xg[i * seg_rows:(i + 1) * seg_rows, :] for i in range(SEG_LEN)]
            wrapped = {}
            for i_src in range(SEG_LEN - CONV_WIDTH + 1, SEG_LEN):
                k = i_src - (SEG_LEN - CONV_WIDTH + 1)
                wrapped[i_src] = shift_rows(x_blk[i_src], xhist_ref[k, :, cg])
                xhist_ref[k, :, cg] = x_blk[i_src][seg_rows - 1:seg_rows, :]
            xc_blk = []
            for i in range(SEG_LEN):
                acc = conv_b_ref[:, cg]
                for d in range(CONV_WIDTH):
                    src = x_blk[i - d] if i >= d else wrapped[i - d + SEG_LEN]
                    acc = acc + src * conv_w_ref[CONV_WIDTH - 1 - d:CONV_WIDTH - d, cg]
                xc_blk.append(acc)
            st["xc"] = jnp.concatenate(xc_blk, axis=0)
            xc_b = st["xc"].astype(bf16)
            st["gates"] = [
                jnp.dot(xc_b[:, k * MXU_DIM:(k + 1) * MXU_DIM], w_gate_ref[g * (GROUP_W // MXU_DIM) + k],
                        preferred_element_type=f32) for k in range(GROUP_W // MXU_DIM)]

        def decay_and_input():
            gates, xc = st["gates"], st["xc"]
            th_r = jnp.tanh(jnp.concatenate([ri[:, :MXU_DIM] for ri in gates], axis=1)
                            + 0.5 * b_a_ref[:, cg])
            th_i = jnp.tanh(jnp.concatenate([ri[:, MXU_DIM:] for ri in gates], axis=1)
                            + 0.5 * b_x_ref[:, cg])
            lam = lam_ref[:, cg]
            softplus_neg_lam = jnp.maximum(-lam, 0.0) + jnp.log1p(jnp.exp(-jnp.abs(lam)))
            k = (0.5 * LRU_C) * softplus_neg_lam
            neg_log_a = th_r * k + k
            a = jnp.exp2(neg_log_a * (-LOG2E))
            t = jnp.tanh(neg_log_a) * (a * a + 1.0)
            mult = jnp.where(t > 0.0, t * lax.rsqrt(t), 0.0)
            half_xc = 0.5 * xc
            st["a"], st["b"] = a, mult * (th_i * half_xc + half_xc)

        def local_scan():
            a, b = st["a"], st["b"]
            a_blk = [a[i * seg_rows:(i + 1) * seg_rows, :] for i in range(SEG_LEN)]
            b_blk = [b[i * seg_rows:(i + 1) * seg_rows, :] for i in range(SEG_LEN)]
            h_loc, a_cum = [b_blk[0]], [a_blk[0]]
            for i in range(1, SEG_LEN):
                h_loc.append(a_blk[i] * h_loc[-1] + b_blk[i])
                a_cum.append(a_blk[i] * a_cum[-1])
            st["h_loc"], st["a_cum"] = h_loc, a_cum

        def carry_and_gate():
            h_loc, a_cum = st["h_loc"], st["a_cum"]
            seg_a, seg_h = a_cum[-1], h_loc[-1]
            shift = 1
            while shift < SUBLANES:
                keep = (seg_row % SUBLANES) >= shift
                a_prev = jnp.where(keep, pltpu.roll(seg_a, shift, axis=0), 1.0)
                h_prev = jnp.where(keep, pltpu.roll(seg_h, shift, axis=0), 0.0)
                seg_h = seg_a * h_prev + seg_h
                seg_a = seg_a * a_prev
                shift *= 2
            carry = hstate_ref[:, cg]
            state_in = carry
            ends = []
            for v in range(seg_rows // SUBLANES):
                lo = v * SUBLANES
                hv = seg_a[lo:lo + SUBLANES, :] * carry + seg_h[lo:lo + SUBLANES, :]
                ends.append(hv)
                carry = hv[SUBLANES - 1:SUBLANES, :]
            hstate_ref[:, cg] = carry
            seg_in = shift_rows(jnp.concatenate(ends, axis=0), state_in)
            h_lru = jnp.concatenate([h_loc[i] + a_cum[i] * seg_in for i in range(SEG_LEN)], axis=0)
            ylru_ref[:, cg] = (h_lru * jax.nn.gelu(proj(C_GATE + g * GROUP_W, unp_ref))).astype(bf16)

        def merge_gates():
            sgl_ref[:, cg] = _sigmoid(proj(C_GL + g * GROUP_W))
            sga_ref[:, cg] = _sigmoid(proj(C_GA + g * GROUP_W))

        return [conv_and_gates, decay_and_input, local_scan, carry_and_gate, merge_gates]

    first_head2 = lax.broadcasted_iota(jnp.int32, (2 * BLOCK, LANES), 1) < HEAD_DIM
    qi = lax.broadcasted_iota(jnp.int32, (2 * BLOCK, 2 * BLOCK), 0) % BLOCK
    si = lax.broadcasted_iota(jnp.int32, (2 * BLOCK, 2 * BLOCK), 1)
    diff = BLOCK + qi - si
    band = (diff >= 0) & (diff < WINDOW)
    first_pair_rows = lax.broadcasted_iota(jnp.int32, (2 * BLOCK, 1), 0) < BLOCK

    def scores(j, qb):
        r0 = qb * BLOCK
        q2 = jnp.concatenate(
            [q_ref[r0:r0 + BLOCK, (2 * j + pp) * LANES:(2 * j + pp + 1) * LANES] for pp in range(2)],
            axis=0)
        mask = band & ((si >= BLOCK) | (t_idx * (ts // BLOCK) + qb > 0))
        halves = []
        for second in range(2):
            slab = slice((2 * j + second) * LANES, (2 * j + second + 1) * LANES)
            head = 4 * j + second
            sink = jnp.where(first_pair_rows, sinks_ref[head] * LOG2E, sinks_ref[head + 2] * LOG2E)
            s = lax.dot_general(q2, kall_ref[r0:r0 + 2 * BLOCK, slab], (((1,), (1,)), ((), ())),
                                preferred_element_type=f32)
            s = jnp.where(mask, s, MASK_VALUE)
            m = jnp.maximum(jnp.max(s, axis=1, keepdims=True), sink)
            halves.append((s, m, sink))
        return halves

    def weighted_values(j, qb, halves):
        r0 = qb * BLOCK
        parts = []
        for second, (s, m, sink) in enumerate(halves):
            slab = slice((2 * j + second) * LANES, (2 * j + second + 1) * LANES)
            e = jnp.exp2(s - m).astype(bf16)
            pv = jnp.dot(e, vall_ref[r0:r0 + 2 * BLOCK, slab], preferred_element_type=f32)
            sum_lane = SUM_LANE_HIGH if second else SUM_LANE_LOW
            denom = pv[:, sum_lane:sum_lane + 1] + jnp.exp2(sink - m)
            parts.append(pv * (1.0 / denom))
        out = jnp.where(first_head2, parts[0], parts[1]).astype(bf16)
        for pp in range(2):
            lo = (2 * j + pp) * LANES
            yattn_ref[r0:r0 + BLOCK, lo:lo + LANES] = out[pp * BLOCK:(pp + 1) * BLOCK, :]

    def attention_stages():
        blocks = [(j, qb) for j in range(N_KV_HEADS) for qb in range(ts // BLOCK)]
        pending = {}

        def stage(n):
            def run():
                if n < len(blocks):
                    pending[n] = scores(*blocks[n])
                if n >= 1:
                    weighted_values(*blocks[n - 1], pending.pop(n - 1))
            return run

        return [stage(n) for n in range(len(blocks) + 1)]

    def lru_projection(c):
        p_lru = jnp.dot(ylru_ref[...], w_proj_ref[:, P_LRU + c * GROUP_W:P_LRU + (c + 1) * GROUP_W],
                        preferred_element_type=f32)
        for half in range(GROUP_W // LANES):
            lg = c * (GROUP_W // LANES) + half
            for i in range(SEG_LEN):
                for grp in range(seg_rows // SUBLANES):
                    row = i * seg_rows + grp * SUBLANES
                    plru_ref.at[lg][pl.ds(SEG_LEN * SUBLANES * grp + i, SUBLANES, stride=SEG_LEN), :] = (
                        p_lru[row:row + SUBLANES, half * LANES:(half + 1) * LANES])

    lru_list = [stage for g in range(GROUPS) for stage in lru_stages(g)]
    lru_list += [lambda c=c: lru_projection(c) for c in range(GROUPS)]
    att_list = attention_stages()
    att_per_lru = -(-len(att_list) // len(lru_list))
    while lru_list or att_list:
        if lru_list:
            lru_list.pop(0)()
        for _ in range(att_per_lru):
            if att_list:
                att_list.pop(0)()

    kall_ref[0:BLOCK, :] = kall_ref[ts:ts + BLOCK, :]
    vall_ref[0:BLOCK, :] = vall_ref[ts:ts + BLOCK, :]

    for n in range(GROUPS):
        cn = slice(n * GROUP_W, (n + 1) * GROUP_W)
        p_attn = jnp.dot(yattn_ref[...], w_proj_ref[:, P_ATTN + n * GROUP_W:P_ATTN + (n + 1) * GROUP_W],
                         preferred_element_type=f32)
        p_lru = jnp.concatenate([plru_ref[n * (GROUP_W // LANES) + half] for half in range(GROUP_W // LANES)],
                                axis=1)
        merged_ref[:, cn] = (sgl_ref[:, cn] * p_lru + sga_ref[:, cn] * p_attn).astype(bf16)
    m_out = [jnp.dot(merged_ref[...], w_proj_ref[:, P_OUT + n * GROUP_W:P_OUT + (n + 1) * GROUP_W],
                     preferred_element_type=f32) for n in range(GROUPS)]
    mean_sq = sum(jnp.sum(mo * mo, axis=1, keepdims=True) for mo in m_out) * (1.0 / D_MODEL)
    inv_rms = lax.rsqrt(mean_sq + NORM_EPS)
    for n in range(GROUPS):
        cn = slice(n * GROUP_W, (n + 1) * GROUP_W)
        o_ref[:, cn] = h_ref[:, cn] + m_out[n] * inv_rms * post_g_ref[:, cn]


def _mixer(h3d, sinks, cos_t, sin_t, pre_g, w_in_p, conv_w, conv_b, w_gate, b_a, b_x, lam,
           w_proj, post_g):
    bsz, s, _ = h3d.shape
    ts = MIX_TOKENS
    tile = pl.BlockSpec((None, ts, D_MODEL), lambda b, t: (b, t, 0))
    h_by_offset = h3d.reshape(bsz, s // SEG_LEN, SEG_LEN * D_MODEL)
    offset_specs = [pl.BlockSpec((None, ts // SEG_LEN, D_MODEL), lambda b, t, i=i: (b, t, i))
                    for i in range(SEG_LEN)]
    rope_spec = pl.BlockSpec((ts, LANES), lambda b, t: (t, 0))
    return pl.pallas_call(
        _mixer_kernel,
        name="mixer",
        grid=(bsz, s // ts),
        in_specs=[
            pl.BlockSpec(memory_space=pltpu.SMEM),
            tile, *offset_specs, rope_spec, rope_spec,
            _const_spec((1, D_MODEL)),
            _const_spec((D_MODEL, IN_WIDTH)),
            _const_spec((CONV_WIDTH, LRU_WIDTH)),
            _const_spec((1, LRU_WIDTH)),
            _const_spec((LRU_WIDTH // MXU_DIM, MXU_DIM, 2 * MXU_DIM)),
            _const_spec((1, LRU_WIDTH)),
            _const_spec((1, LRU_WIDTH)),
            _const_spec((1, LRU_WIDTH)),
            _const_spec((D_MODEL, 3 * D_MODEL + PAD_COLS)),
            _const_spec((1, D_MODEL)),
        ],
        out_specs=tile,
        out_shape=jax.ShapeDtypeStruct(h3d.shape, jnp.float32),
        scratch_shapes=[
            pltpu.VMEM((ts, D_MODEL), jnp.bfloat16),
            pltpu.VMEM((ts, D_MODEL), jnp.bfloat16),
            pltpu.VMEM((CONV_WIDTH - 1, 1, LRU_WIDTH), jnp.float32),
            pltpu.VMEM((1, LRU_WIDTH), jnp.float32),
            pltpu.VMEM((ts, ATTN_WIDTH), jnp.bfloat16),
            pltpu.VMEM((BLOCK + ts, KV_SLAB_WIDTH), jnp.bfloat16),
            pltpu.VMEM((BLOCK + ts, KV_SLAB_WIDTH), jnp.bfloat16),
            pltpu.VMEM((ts, LRU_WIDTH), jnp.bfloat16),
            pltpu.VMEM((ts, ATTN_WIDTH), jnp.bfloat16),
            pltpu.VMEM((ts, D_MODEL), jnp.float32),
            pltpu.VMEM((ts, D_MODEL), jnp.float32),
            pltpu.VMEM((D_MODEL // LANES, ts, LANES), jnp.float32),
            pltpu.VMEM((ts, D_MODEL), jnp.bfloat16),
        ],
        compiler_params=pltpu.CompilerParams(
            dimension_semantics=("arbitrary", "arbitrary"), vmem_limit_bytes=VMEM_LIMIT_BYTES),
    )(sinks, h3d, *([h_by_offset] * SEG_LEN), cos_t, sin_t, pre_g, w_in_p, conv_w, conv_b, w_gate, b_a, b_x, lam,
      w_proj, post_g)


def _pad_cast(*ws):
    zeros = jnp.zeros((ws[0].shape[0], PAD_COLS), jnp.bfloat16)
    return jnp.concatenate([w.astype(jnp.bfloat16) for w in ws] + [zeros], axis=1)


def _gate_weights(w_a, w_x):
    per = MXU_DIM // LRU_BLOCK_W
    eye = jnp.eye(per, dtype=w_a.dtype)

    def expand(w):
        w4 = w.reshape(LRU_BLOCKS // per, per, LRU_BLOCK_W, LRU_BLOCK_W)
        return jnp.einsum("gncd,nm->gncmd", w4, eye).reshape(LRU_BLOCKS // per, MXU_DIM, MXU_DIM)

    return (0.5 * jnp.concatenate([expand(w_a), expand(w_x)], axis=2)).astype(jnp.bfloat16)


def _rope_tables(seq_len):
    inv_freq = ROPE_THETA ** (-np.arange(HALF, dtype=np.float64) / HALF)
    ang = np.arange(seq_len, dtype=np.float64)[:, None] * inv_freq[None, :]
    cos, sin = np.cos(ang), np.sin(ang)
    return (jnp.asarray(np.concatenate([cos, cos, cos, cos], axis=1), jnp.float32),
            jnp.asarray(np.concatenate([-sin, sin, -sin, sin], axis=1), jnp.float32))


def kernel(x, ffn1_pre_g, ffn1_w_gu, ffn1_w_down, ffn1_post_g, mix_pre_g, w_in, conv_w, conv_b, lru_w_a, lru_b_a, lru_w_x, lru_b_x, lru_lambda, attn_sinks, w_proj_lru, w_proj_attn, w_out, mix_post_g, ffn2_pre_g, ffn2_w_gu, ffn2_w_down, ffn2_post_g):
    bsz, s, d = x.shape
    depth = ffn1_pre_g.shape[0]
    cos_t, sin_t = _rope_tables(s)
    bf = jnp.bfloat16
    h = x
    for l in range(depth):
        h = _ffn(h.reshape(bsz * s, d), ffn1_pre_g[l][None], ffn1_w_gu[l], ffn1_w_down[l],
                 ffn1_post_g[l][None]).reshape(bsz, s, d)
        h = _mixer(h, attn_sinks[l], cos_t, sin_t, mix_pre_g[l][None], w_in[l].astype(bf),
                   conv_w[l], conv_b[l][None], _gate_weights(lru_w_a[l], lru_w_x[l]),
                   lru_b_a[l][None], lru_b_x[l][None], lru_lambda[l][None],
                   _pad_cast(w_proj_lru[l], w_proj_attn[l], w_out[l]),
                   mix_post_g[l][None])
        h = _ffn(h.reshape(bsz * s, d), ffn2_pre_g[l][None], ffn2_w_gu[l], ffn2_w_down[l],
                 ffn2_post_g[l][None]).reshape(bsz, s, d)
    return h
```

```python
import jax
import jax.numpy as jnp
import numpy as np
from jax import lax
from jax.experimental import pallas as pl
from jax.experimental.pallas import tpu as pltpu

D_MODEL = 1024
LRU_WIDTH = 1024
LRU_BLOCKS = 16
LRU_BLOCK_W = LRU_WIDTH // LRU_BLOCKS
CONV_WIDTH = 4
LRU_C = 8.0
HEAD_DIM = 64
HALF = HEAD_DIM // 2
N_Q_HEADS = 16
N_KV_HEADS = 4
ATTN_WIDTH = N_Q_HEADS * HEAD_DIM
KV_WIDTH = N_KV_HEADS * HEAD_DIM
WINDOW = 128
BLOCK = 128
ROPE_THETA = 10000.0
D_FF = 2816
MACARON_SCALE = 0.5
NORM_EPS = 1e-6
MASK_VALUE = -1e30

LANES = 128
SUBLANES = 8
MXU_DIM = 256
VMEM_LIMIT_BYTES = 56 * 1024 * 1024

FFN_TOKENS = 512
FFN_CHUNK = 256
DOWN_CAST_ROWS = 176
MIX_TOKENS = 512
SEG_LEN = SUBLANES

GROUP_W = 2 * MXU_DIM
GROUPS = LRU_WIDTH // GROUP_W
C_GATE = 0
C_X = C_GATE + LRU_WIDTH
C_Q = C_X + LRU_WIDTH
C_K = C_Q + ATTN_WIDTH
C_V = C_K + KV_WIDTH
C_GL = C_V + KV_WIDTH
C_GA = C_GL + D_MODEL
IN_WIDTH = C_GA + D_MODEL
LOG2E = 1.4426950408889634
P_LRU, P_ATTN, P_OUT = 0, D_MODEL, 2 * D_MODEL
KV_SLAB_WIDTH = N_KV_HEADS * 2 * LANES
SUM_LANE_LOW, SUM_LANE_HIGH = HEAD_DIM, 0
PAD_COLS = LANES


def _rms_norm(x, g):
    return x * lax.rsqrt(jnp.mean(x * x, axis=-1, keepdims=True) + NORM_EPS) * g


def _sigmoid(x):
    return 0.5 * jnp.tanh(0.5 * x) + 0.5


def _const_spec(shape):
    n = len(shape)
    return pl.BlockSpec(shape, lambda *_: (0,) * n, pipeline_mode=pl.Buffered(1))


def _ffn_kernel(h_ref, pre_g_ref, w_gu_hbm, w_down_hbm, post_g_ref, o_ref,
                w_gu_ref, w_down_ref, stage_gu_ref, stage_down_ref, sem_gu_ref, sem_down_ref, act_ref):
    bf16, f32 = jnp.bfloat16, jnp.float32
    n_chunks = D_FF // FFN_CHUNK

    def gu_copies(c):
        slot = c % 2
        return [pltpu.make_async_copy(w_gu_hbm.at[:, pl.ds(part * D_FF + c * FFN_CHUNK, FFN_CHUNK)],
                                      stage_gu_ref.at[slot, part], sem_gu_ref.at[2 * slot + part])
                for part in range(2)]

    down_copy = pltpu.make_async_copy(w_down_hbm, stage_down_ref, sem_down_ref.at[0])

    def body(stage_weights):
        if stage_weights:
            for part, cp in enumerate(gu_copies(0)):
                cp.start(priority=part)
            down_copy.start()
        xn = _rms_norm(h_ref[...], pre_g_ref[...]).astype(bf16)
        for c in range(n_chunks):
            lo = c * FFN_CHUNK
            if stage_weights:
                if c + 1 < n_chunks:
                    for part, cp in enumerate(gu_copies(c + 1)):
                        cp.start(priority=part)
                for part, cp in enumerate(gu_copies(c)):
                    cp.wait()
                    col = part * D_FF + lo
                    w_gu_ref[:, col:col + FFN_CHUNK] = stage_gu_ref[c % 2, part].astype(bf16)
            g = jnp.dot(xn, w_gu_ref[:, lo:lo + FFN_CHUNK], preferred_element_type=f32)
            u = jnp.dot(xn, w_gu_ref[:, D_FF + lo:D_FF + lo + FFN_CHUNK], preferred_element_type=f32)
            act_ref[:, lo:lo + FFN_CHUNK] = (jax.nn.silu(g) * u).astype(bf16)
        if stage_weights:
            down_copy.wait()
            for r in range(0, D_FF, DOWN_CAST_ROWS):
                w_down_ref[r:r + DOWN_CAST_ROWS, 0:D_MODEL] = (
                    stage_down_ref[r:r + DOWN_CAST_ROWS, :].astype(bf16))
        f = [jnp.dot(act_ref[...], w_down_ref[:, n * GROUP_W:(n + 1) * GROUP_W],
                     preferred_element_type=f32) for n in range(D_MODEL // GROUP_W)]
        mean_sq = sum(jnp.sum(fn * fn, axis=1, keepdims=True) for fn in f) * (1.0 / D_MODEL)
        scale = MACARON_SCALE * lax.rsqrt(mean_sq + NORM_EPS)
        for n in range(D_MODEL // GROUP_W):
            cn = slice(n * GROUP_W, (n + 1) * GROUP_W)
            o_ref[:, cn] = h_ref[:, cn] + f[n] * scale * post_g_ref[:, cn]

    first = pl.program_id(0) == 0
    pl.when(first)(lambda: body(True))
    pl.when(jnp.logical_not(first))(lambda: body(False))


def _ffn(h2d, pre_g, w_gu, w_down, post_g):
    m = h2d.shape[0]
    tile = pl.BlockSpec((FFN_TOKENS, D_MODEL), lambda i: (i, 0))
    return pl.pallas_call(
        _ffn_kernel,
        name="ffn",
        grid=(m // FFN_TOKENS,),
        in_specs=[
            tile,
            _const_spec((1, D_MODEL)),
            pl.BlockSpec(memory_space=pl.ANY),
            pl.BlockSpec(memory_space=pl.ANY),
            _const_spec((1, D_MODEL)),
        ],
        out_specs=tile,
        out_shape=jax.ShapeDtypeStruct((m, D_MODEL), jnp.float32),
        scratch_shapes=[
            pltpu.VMEM((D_MODEL, 2 * D_FF), jnp.bfloat16),
            pltpu.VMEM((D_FF, D_MODEL + PAD_COLS), jnp.bfloat16),
            pltpu.VMEM((2, 2, D_MODEL, FFN_CHUNK), jnp.float32),
            pltpu.VMEM((D_FF, D_MODEL), jnp.float32),
            pltpu.SemaphoreType.DMA((4,)),
            pltpu.SemaphoreType.DMA((1,)),
            pltpu.VMEM((FFN_TOKENS, D_FF), jnp.bfloat16),
        ],
        compiler_params=pltpu.CompilerParams(
            dimension_semantics=("arbitrary",), vmem_limit_bytes=VMEM_LIMIT_BYTES),
    )(h2d, pre_g, w_gu, w_down, post_g)


def _mixer_kernel(sinks_ref, h_ref, cos_ref, sin_ref, pre_g_ref, w_in_ref, conv_w_ref,
                  conv_b_ref, w_gate_ref, b_a_ref, b_x_ref, lam_ref, w_proj_ref, post_g_ref, o_ref,
                  un_ref, unf_ref, unp_ref, xhist_ref, hstate_ref, q_ref, kall_ref, vall_ref, ylru_ref,
                  yattn_ref, sgl_ref, sga_ref, plru_ref, merged_ref):
    ts = MIX_TOKENS
    seg_rows = ts // SEG_LEN
    t_idx = pl.program_id(1)
    f32, bf16 = jnp.float32, jnp.bfloat16

    @pl.when(t_idx == 0)
    def _():
        xhist_ref[...] = jnp.zeros_like(xhist_ref)
        hstate_ref[...] = jnp.zeros_like(hstate_ref)
        kall_ref[0:BLOCK, :] = jnp.zeros((BLOCK, KV_SLAB_WIDTH), bf16)
        vall_ref[0:BLOCK, :] = jnp.zeros((BLOCK, KV_SLAB_WIDTH), bf16)

    un = _rms_norm(h_ref[...], pre_g_ref[...])
    un_ref[...] = un.astype(bf16)
    for lg in range(D_MODEL // LANES):
        unf_ref[lg] = un[:, lg * LANES:(lg + 1) * LANES]
    for lg in range(D_MODEL // LANES):
        pieces = [unf_ref.at[lg][pl.ds(SEG_LEN * SUBLANES * grp + i, SUBLANES, stride=SEG_LEN), :]
                  for i in range(SEG_LEN) for grp in range(seg_rows // SUBLANES)]
        unp_ref[:, lg * LANES:(lg + 1) * LANES] = jnp.concatenate(pieces, axis=0).astype(bf16)

    def proj(lo, src_ref=un_ref):
        return jnp.dot(src_ref[...], w_in_ref[:, lo:lo + GROUP_W], preferred_element_type=f32)

    lane = lax.broadcasted_iota(jnp.int32, (ts, LANES), 1)
    first_half = (lane % HEAD_DIM) < HALF
    first_head = lane < HEAD_DIM

    def rope(x, c, s):
        partner = jnp.where(first_half, pltpu.roll(x, LANES - HALF, axis=1),
                            pltpu.roll(x, HALF, axis=1))
        return x * c + partner * s

    cos = cos_ref[...]
    sin = sin_ref[...]
    q_scale = (HEAD_DIM ** -0.5) * LOG2E
    cos_q = cos * q_scale
    sin_q = sin * q_scale
    seg_row = lax.broadcasted_iota(jnp.int32, (seg_rows, GROUP_W), 0)

    def shift_rows(cur, first_row):
        return jnp.where(seg_row == 0, first_row, pltpu.roll(cur, 1, axis=0))

    kv_nat = proj(C_K)
    k_nat, v_nat = kv_nat[:, :KV_WIDTH], kv_nat[:, KV_WIDTH:]
    for jj in range(N_KV_HEADS // 2):
        kr = rope(k_nat[:, jj * LANES:(jj + 1) * LANES], cos, sin)
        vv = v_nat[:, jj * LANES:(jj + 1) * LANES]
        kr_swapped = pltpu.roll(kr, HEAD_DIM, axis=1)
        vv_swapped = pltpu.roll(vv, HEAD_DIM, axis=1)
        for half in range(2):
            j = 2 * jj + half
            for dst_ref, same, swapped, fill_low, fill_high in (
                    (kall_ref, kr, kr_swapped, 0.0, 0.0),
                    (vall_ref, vv, vv_swapped, (lane == SUM_LANE_LOW).astype(f32),
                     (lane == SUM_LANE_HIGH).astype(f32))):
                low, high = (same, swapped) if half == 0 else (swapped, same)
                dst_ref[BLOCK:BLOCK + ts, (2 * j) * LANES:(2 * j + 1) * LANES] = (
                    jnp.where(first_head, low, fill_low).astype(bf16))
                dst_ref[BLOCK:BLOCK + ts, (2 * j + 1) * LANES:(2 * j + 2) * LANES] = (
                    jnp.where(first_head, fill_high, high).astype(bf16))

    for g in range(GROUPS):
        qg = proj(C_Q + g * GROUP_W)
        for pp in range(GROUP_W // LANES):
            lo = g * GROUP_W + pp * LANES
            q_ref[:, lo:lo + LANES] = rope(qg[:, pp * LANES:(pp + 1) * LANES], cos_q, sin_q).astype(bf16)

    def lru_stages(g):
        cg = slice(g * GROUP_W, (g + 1) * GROUP_W)
        st = {}

        def conv_and_gates():
            xg = proj(C_X + g * GROUP_W, unp_ref)
            x_blk = [xg[i * seg_rows:(i + 1) * seg_rows, :] for i in range(SEG_LEN)]
            wrapped = {}
            for i_src in range(SEG_LEN - CONV_WIDTH + 1, SEG_LEN):
                k = i_src - (SEG_LEN - CONV_WIDTH + 1)
                wrapped[i_src] = shift_rows(x_blk[i_src], xhist_ref[k, :, cg])
                xhist_ref[k, :, cg] = x_blk[i_src][seg_rows - 1:seg_rows, :]
            xc_blk = []
            for i in range(SEG_LEN):
                acc = conv_b_ref[:, cg]
                for d in range(CONV_WIDTH):
                    src = x_blk[i - d] if i >= d else wrapped[i - d + SEG_LEN]
                    acc = acc + src * conv_w_ref[CONV_WIDTH - 1 - d:CONV_WIDTH - d, cg]
                xc_blk.append(acc)
            st["xc"] = jnp.concatenate(xc_blk, axis=0)
            xc_b = st["xc"].astype(bf16)
            st["gates"] = [
                jnp.dot(xc_b[:, k * MXU_DIM:(k + 1) * MXU_DIM], w_gate_ref[g * (GROUP_W // MXU_DIM) + k],
                        preferred_element_type=f32) for k in range(GROUP_W // MXU_DIM)]

        def decay_and_input():
            gates, xc = st["gates"], st["xc"]
            th_r = jnp.tanh(jnp.concatenate([ri[:, :MXU_DIM] for ri in gates], axis=1)
                            + 0.5 * b_a_ref[:, cg])
            th_i = jnp.tanh(jnp.concatenate([ri[:, MXU_DIM:] for ri in gates], axis=1)
                            + 0.5 * b_x_ref[:, cg])
            lam = lam_ref[:, cg]
            softplus_neg_lam = jnp.maximum(-lam, 0.0) + jnp.log1p(jnp.exp(-jnp.abs(lam)))
            k = (0.5 * LRU_C) * softplus_neg_lam
            neg_log_a = th_r * k + k
            a = jnp.exp2(neg_log_a * (-LOG2E))
            t = jnp.tanh(neg_log_a) * (a * a + 1.0)
            mult = jnp.where(t > 0.0, t * lax.rsqrt(t), 0.0)
            half_xc = 0.5 * xc
            st["a"], st["b"] = a, mult * (th_i * half_xc + half_xc)

        def local_scan():
            a, b = st["a"], st["b"]
            a_blk = [a[i * seg_rows:(i + 1) * seg_rows, :] for i in range(SEG_LEN)]
            b_blk = [b[i * seg_rows:(i + 1) * seg_rows, :] for i in range(SEG_LEN)]
            h_loc, a_cum = [b_blk[0]], [a_blk[0]]
            for i in range(1, SEG_LEN):
                h_loc.append(a_blk[i] * h_loc[-1] + b_blk[i])
                a_cum.append(a_blk[i] * a_cum[-1])
            st["h_loc"], st["a_cum"] = h_loc, a_cum

        def carry_and_gate():
            h_loc, a_cum = st["h_loc"], st["a_cum"]
            seg_a, seg_h = a_cum[-1], h_loc[-1]
            shift = 1
            while shift < SUBLANES:
                keep = (seg_row % SUBLANES) >= shift
                a_prev = jnp.where(keep, pltpu.roll(seg_a, shift, axis=0), 1.0)
                h_prev = jnp.where(keep, pltpu.roll(seg_h, shift, axis=0), 0.0)
                seg_h = seg_a * h_prev + seg_h
                seg_a = seg_a * a_prev
                shift *= 2
            carry = hstate_ref[:, cg]
            state_in = carry
            ends = []
            for v in range(seg_rows // SUBLANES):
                lo = v * SUBLANES
                hv = seg_a[lo:lo + SUBLANES, :] * carry + seg_h[lo:lo + SUBLANES, :]
                ends.append(hv)
                carry = hv[SUBLANES - 1:SUBLANES, :]
            hstate_ref[:, cg] = carry
            seg_in = shift_rows(jnp.concatenate(ends, axis=0), state_in)
            h_lru = jnp.concatenate([h_loc[i] + a_cum[i] * seg_in for i in range(SEG_LEN)], axis=0)
            ylru_ref[:, cg] = (h_lru * jax.nn.gelu(proj(C_GATE + g * GROUP_W, unp_ref))).astype(bf16)

        def merge_gates():
            sgl_ref[:, cg] = _sigmoid(proj(C_GL + g * GROUP_W))
            sga_ref[:, cg] = _sigmoid(proj(C_GA + g * GROUP_W))

        return [conv_and_gates, decay_and_input, local_scan, carry_and_gate, merge_gates]

    first_head2 = lax.broadcasted_iota(jnp.int32, (2 * BLOCK, LANES), 1) < HEAD_DIM
    qi = lax.broadcasted_iota(jnp.int32, (2 * BLOCK, 2 * BLOCK), 0) % BLOCK
    si = lax.broadcasted_iota(jnp.int32, (2 * BLOCK, 2 * BLOCK), 1)
    diff = BLOCK + qi - si
    band = (diff >= 0) & (diff < WINDOW)
    first_pair_rows = lax.broadcasted_iota(jnp.int32, (2 * BLOCK, 1), 0) < BLOCK

    def scores(j, qb):
        r0 = qb * BLOCK
        q2 = jnp.concatenate(
            [q_ref[r0:r0 + BLOCK, (2 * j + pp) * LANES:(2 * j + pp + 1) * LANES] for pp in range(2)],
            axis=0)
        mask = band & ((si >= BLOCK) | (t_idx * (ts // BLOCK) + qb > 0))
        halves = []
        for second in range(2):
            slab = slice((2 * j + second) * LANES, (2 * j + second + 1) * LANES)
            head = 4 * j + second
            sink = jnp.where(first_pair_rows, sinks_ref[head] * LOG2E, sinks_ref[head + 2] * LOG2E)
            s = lax.dot_general(q2, kall_ref[r0:r0 + 2 * BLOCK, slab], (((1,), (1,)), ((), ())),
                                preferred_element_type=f32)
            s = jnp.where(mask, s, MASK_VALUE)
            m = jnp.maximum(jnp.max(s, axis=1, keepdims=True), sink)
            halves.append((s, m, sink))
        return halves

    def weighted_values(j, qb, halves):
        r0 = qb * BLOCK
        parts = []
        for second, (s, m, sink) in enumerate(halves):
            slab = slice((2 * j + second) * LANES, (2 * j + second + 1) * LANES)
            e = jnp.exp2(s - m).astype(bf16)
            pv = jnp.dot(e, vall_ref[r0:r0 + 2 * BLOCK, slab], preferred_element_type=f32)
            sum_lane = SUM_LANE_HIGH if second else SUM_LANE_LOW
            denom = pv[:, sum_lane:sum_lane + 1] + jnp.exp2(sink - m)
            parts.append(pv * (1.0 / denom))
        out = jnp.where(first_head2, parts[0], parts[1]).astype(bf16)
        for pp in range(2):
            lo = (2 * j + pp) * LANES
            yattn_ref[r0:r0 + BLOCK, lo:lo + LANES] = out[pp * BLOCK:(pp + 1) * BLOCK, :]

    def attention_stages():
        blocks = [(j, qb) for j in range(N_KV_HEADS) for qb in range(ts // BLOCK)]
        pending = {}

        def stage(n):
            def run():
                if n < len(blocks):
                    pending[n] = scores(*blocks[n])
                if n >= 1:
                    weighted_values(*blocks[n - 1], pending.pop(n - 1))
            return run

        return [stage(n) for n in range(len(blocks) + 1)]

    def lru_projection(c):
        p_lru = jnp.dot(ylru_ref[...], w_proj_ref[:, P_LRU + c * GROUP_W:P_LRU + (c + 1) * GROUP_W],
                        preferred_element_type=f32)
        for half in range(GROUP_W // LANES):
            lg = c * (GROUP_W // LANES) + half
            for i in range(SEG_LEN):
                for grp in range(seg_rows // SUBLANES):
                    row = i * seg_rows + grp * SUBLANES
                    plru_ref.at[lg][pl.ds(SEG_LEN * SUBLANES * grp + i, SUBLANES, stride=SEG_LEN), :] = (
                        p_lru[row:row + SUBLANES, half * LANES:(half + 1) * LANES])

    lru_list = [stage for g in range(GROUPS) for stage in lru_stages(g)]
    lru_list += [lambda c=c: lru_projection(c) for c in range(GROUPS)]
    att_list = attention_stages()
    att_per_lru = -(-len(att_list) // len(lru_list))
    while lru_list or att_list:
        if lru_list:
            lru_list.pop(0)()
        for _ in range(att_per_lru):
            if att_list:
                att_list.pop(0)()

    kall_ref[0:BLOCK, :] = kall_ref[ts:ts + BLOCK, :]
    vall_ref[0:BLOCK, :] = vall_ref[ts:ts + BLOCK, :]

    for n in range(GROUPS):
        cn = slice(n * GROUP_W, (n + 1) * GROUP_W)
        p_attn = jnp.dot(yattn_ref[...], w_proj_ref[:, P_ATTN + n * GROUP_W:P_ATTN + (n + 1) * GROUP_W],
                         preferred_element_type=f32)
        p_lru = jnp.concatenate([plru_ref[n * (GROUP_W // LANES) + half] for half in range(GROUP_W // LANES)],
                                axis=1)
        merged_ref[:, cn] = (sgl_ref[:, cn] * p_lru + sga_ref[:, cn] * p_attn).astype(bf16)
    m_out = [jnp.dot(merged_ref[...], w_proj_ref[:, P_OUT + n * GROUP_W:P_OUT + (n + 1) * GROUP_W],
                     preferred_element_type=f32) for n in range(GROUPS)]
    mean_sq = sum(jnp.sum(mo * mo, axis=1, keepdims=True) for mo in m_out) * (1.0 / D_MODEL)
    inv_rms = lax.rsqrt(mean_sq + NORM_EPS)
    for n in range(GROUPS):
        cn = slice(n * GROUP_W, (n + 1) * GROUP_W)
        o_ref[:, cn] = h_ref[:, cn] + m_out[n] * inv_rms * post_g_ref[:, cn]


def _mixer(h3d, sinks, cos_t, sin_t, pre_g, w_in_p, conv_w, conv_b, w_gate, b_a, b_x, lam,
           w_proj, post_g):
    bsz, s, _ = h3d.shape
    ts = MIX_TOKENS
    tile = pl.BlockSpec((None, ts, D_MODEL), lambda b, t: (b, t, 0))
    rope_spec = pl.BlockSpec((ts, LANES), lambda b, t: (t, 0))
    return pl.pallas_call(
        _mixer_kernel,
        name="mixer",
        grid=(bsz, s // ts),
        in_specs=[
            pl.BlockSpec(memory_space=pltpu.SMEM),
            tile, rope_spec, rope_spec,
            _const_spec((1, D_MODEL)),
            _const_spec((D_MODEL, IN_WIDTH)),
            _const_spec((CONV_WIDTH, LRU_WIDTH)),
            _const_spec((1, LRU_WIDTH)),
            _const_spec((LRU_WIDTH // MXU_DIM, MXU_DIM, 2 * MXU_DIM)),
            _const_spec((1, LRU_WIDTH)),
            _const_spec((1, LRU_WIDTH)),
            _const_spec((1, LRU_WIDTH)),
            _const_spec((D_MODEL, 3 * D_MODEL + PAD_COLS)),
            _const_spec((1, D_MODEL)),
        ],
        out_specs=tile,
        out_shape=jax.ShapeDtypeStruct(h3d.shape, jnp.float32),
        scratch_shapes=[
            pltpu.VMEM((ts, D_MODEL), jnp.bfloat16),
            pltpu.VMEM((D_MODEL // LANES, ts, LANES), jnp.float32),
            pltpu.VMEM((ts, D_MODEL), jnp.bfloat16),
            pltpu.VMEM((CONV_WIDTH - 1, 1, LRU_WIDTH), jnp.float32),
            pltpu.VMEM((1, LRU_WIDTH), jnp.float32),
            pltpu.VMEM((ts, ATTN_WIDTH), jnp.bfloat16),
            pltpu.VMEM((BLOCK + ts, KV_SLAB_WIDTH), jnp.bfloat16),
            pltpu.VMEM((BLOCK + ts, KV_SLAB_WIDTH), jnp.bfloat16),
            pltpu.VMEM((ts, LRU_WIDTH), jnp.bfloat16),
            pltpu.VMEM((ts, ATTN_WIDTH), jnp.bfloat16),
            pltpu.VMEM((ts, D_MODEL), jnp.float32),
            pltpu.VMEM((ts, D_MODEL), jnp.float32),
            pltpu.VMEM((D_MODEL // LANES, ts, LANES), jnp.float32),
            pltpu.VMEM((ts, D_MODEL), jnp.bfloat16),
        ],
        compiler_params=pltpu.CompilerParams(
            dimension_semantics=("arbitrary", "arbitrary"), vmem_limit_bytes=VMEM_LIMIT_BYTES),
    )(sinks, h3d, cos_t, sin_t, pre_g, w_in_p, conv_w, conv_b, w_gate, b_a, b_x, lam,
      w_proj, post_g)


def _pad_cast(*ws):
    zeros = jnp.zeros((ws[0].shape[0], PAD_COLS), jnp.bfloat16)
    return jnp.concatenate([w.astype(jnp.bfloat16) for w in ws] + [zeros], axis=1)


def _gate_weights(w_a, w_x):
    per = MXU_DIM // LRU_BLOCK_W
    eye = jnp.eye(per, dtype=w_a.dtype)

    def expand(w):
        w4 = w.reshape(LRU_BLOCKS // per, per, LRU_BLOCK_W, LRU_BLOCK_W)
        return jnp.einsum("gncd,nm->gncmd", w4, eye).reshape(LRU_BLOCKS // per, MXU_DIM, MXU_DIM)

    return (0.5 * jnp.concatenate([expand(w_a), expand(w_x)], axis=2)).astype(jnp.bfloat16)


def _rope_tables(seq_len):
    inv_freq = ROPE_THETA ** (-np.arange(HALF, dtype=np.float64) / HALF)
    ang = np.arange(seq_len, dtype=np.float64)[:, None] * inv_freq[None, :]
    cos, sin = np.cos(ang), np.sin(ang)
    return (jnp.asarray(np.concatenate([cos, cos, cos, cos], axis=1), jnp.float32),
            jnp.asarray(np.concatenate([-sin, sin, -sin, sin], axis=1), jnp.float32))


def kernel(x, ffn1_pre_g, ffn1_w_gu, ffn1_w_down, ffn1_post_g, mix_pre_g, w_in, conv_w, conv_b, lru_w_a, lru_b_a, lru_w_x, lru_b_x, lru_lambda, attn_sinks, w_proj_lru, w_proj_attn, w_out, mix_post_g, ffn2_pre_g, ffn2_w_gu, ffn2_w_down, ffn2_post_g):
    bsz, s, d = x.shape
    depth = ffn1_pre_g.shape[0]
    cos_t, sin_t = _rope_tables(s)
    bf = jnp.bfloat16
    h = x
    for l in range(depth):
        h = _ffn(h.reshape(bsz * s, d), ffn1_pre_g[l][None], ffn1_w_gu[l], ffn1_w_down[l],
                 ffn1_post_g[l][None]).reshape(bsz, s, d)
        h = _mixer(h, attn_sinks[l], cos_t, sin_t, mix_pre_g[l][None], w_in[l].astype(bf),
                   conv_w[l], conv_b[l][None], _gate_weights(lru_w_a[l], lru_w_x[l]),
                   lru_b_a[l][None], lru_b_x[l][None], lru_lambda[l][None],
                   _pad_cast(w_proj_lru[l], w_proj_attn[l], w_out[l]),
                   mix_post_g[l][None])
        h = _ffn(h.reshape(bsz * s, d), ffn2_pre_g[l][None], ffn2_w_gu[l], ffn2_w_down[l],
                 ffn2_post_g[l][None]).reshape(bsz, s, d)
    return h
```

```python
import jax
import jax.numpy as jnp
import numpy as np
from jax import lax
from jax.experimental import pallas as pl
from jax.experimental.pallas import tpu as pltpu

D_MODEL = 1024
LRU_WIDTH = 1024
LRU_BLOCKS = 16
LRU_BLOCK_W = LRU_WIDTH // LRU_BLOCKS
CONV_WIDTH = 4
LRU_C = 8.0
HEAD_DIM = 64
HALF = HEAD_DIM // 2
N_Q_HEADS = 16
N_KV_HEADS = 4
ATTN_WIDTH = N_Q_HEADS * HEAD_DIM
KV_WIDTH = N_KV_HEADS * HEAD_DIM
WINDOW = 128
BLOCK = 128
ROPE_THETA = 10000.0
D_FF = 2816
MACARON_SCALE = 0.5
NORM_EPS = 1e-6
MASK_VALUE = -1e30

LANES = 128
SUBLANES = 8
MXU_DIM = 256
VMEM_LIMIT_BYTES = 56 * 1024 * 1024

FFN_TOKENS = 512
FFN_CHUNK = 256
DOWN_CAST_ROWS = 176
MIX_TOKENS = 512
SEG_LEN = SUBLANES

GROUP_W = 2 * MXU_DIM
GROUPS = LRU_WIDTH // GROUP_W
C_GATE = 0
C_X = C_GATE + LRU_WIDTH
C_Q = C_X + LRU_WIDTH
C_K = C_Q + ATTN_WIDTH
C_V = C_K + KV_WIDTH
C_GL = C_V + KV_WIDTH
C_GA = C_GL + D_MODEL
IN_WIDTH = C_GA + D_MODEL
LOG2E = 1.4426950408889634
P_LRU, P_ATTN, P_OUT = 0, D_MODEL, 2 * D_MODEL
KV_SLAB_WIDTH = N_KV_HEADS * 2 * LANES
SUM_LANE_LOW, SUM_LANE_HIGH = HEAD_DIM, 0
PAD_COLS = LANES


def _rms_norm(x, g):
    return x * lax.rsqrt(jnp.mean(x * x, axis=-1, keepdims=True) + NORM_EPS) * g


def _const_spec(shape):
    n = len(shape)
    return pl.BlockSpec(shape, lambda *_: (0,) * n, pipeline_mode=pl.Buffered(1))


def _ffn_kernel(h_ref, pre_g_ref, w_gu_hbm, w_down_hbm, post_g_ref, o_ref,
                w_gu_ref, w_down_ref, stage_gu_ref, stage_down_ref, sem_gu_ref, sem_down_ref, act_ref):
    bf16, f32 = jnp.bfloat16, jnp.float32
    n_chunks = D_FF // FFN_CHUNK

    def gu_copies(c):
        slot = c % 2
        return [pltpu.make_async_copy(w_gu_hbm.at[:, pl.ds(part * D_FF + c * FFN_CHUNK, FFN_CHUNK)],
                                      stage_gu_ref.at[slot, part], sem_gu_ref.at[2 * slot + part])
                for part in range(2)]

    down_copy = pltpu.make_async_copy(w_down_hbm, stage_down_ref, sem_down_ref.at[0])

    def body(stage_weights):
        if stage_weights:
            for part, cp in enumerate(gu_copies(0)):
                cp.start(priority=part)
            down_copy.start()
        xn = _rms_norm(h_ref[...], pre_g_ref[...]).astype(bf16)
        for c in range(n_chunks):
            lo = c * FFN_CHUNK
            if stage_weights:
                if c + 1 < n_chunks:
                    for part, cp in enumerate(gu_copies(c + 1)):
                        cp.start(priority=part)
                for part, cp in enumerate(gu_copies(c)):
                    cp.wait()
                    col = part * D_FF + lo
                    w_gu_ref[:, col:col + FFN_CHUNK] = stage_gu_ref[c % 2, part].astype(bf16)
            g = jnp.dot(xn, w_gu_ref[:, lo:lo + FFN_CHUNK], preferred_element_type=f32)
            u = jnp.dot(xn, w_gu_ref[:, D_FF + lo:D_FF + lo + FFN_CHUNK], preferred_element_type=f32)
            act_ref[:, lo:lo + FFN_CHUNK] = (jax.nn.silu(g) * u).astype(bf16)
        if stage_weights:
            down_copy.wait()
            for r in range(0, D_FF, DOWN_CAST_ROWS):
                w_down_ref[r:r + DOWN_CAST_ROWS, 0:D_MODEL] = (
                    stage_down_ref[r:r + DOWN_CAST_ROWS, :].astype(bf16))
        f = [jnp.dot(act_ref[...], w_down_ref[:, n * GROUP_W:(n + 1) * GROUP_W],
                     preferred_element_type=f32) for n in range(D_MODEL // GROUP_W)]
        mean_sq = sum(jnp.sum(fn * fn, axis=1, keepdims=True) for fn in f) * (1.0 / D_MODEL)
        scale = MACARON_SCALE * lax.rsqrt(mean_sq + NORM_EPS)
        for n in range(D_MODEL // GROUP_W):
            cn = slice(n * GROUP_W, (n + 1) * GROUP_W)
            o_ref[:, cn] = h_ref[:, cn] + f[n] * scale * post_g_ref[:, cn]

    first = pl.program_id(0) == 0
    pl.when(first)(lambda: body(True))
    pl.when(jnp.logical_not(first))(lambda: body(False))


def _ffn(h2d, pre_g, w_gu, w_down, post_g):
    m = h2d.shape[0]
    tile = pl.BlockSpec((FFN_TOKENS, D_MODEL), lambda i: (i, 0))
    return pl.pallas_call(
        _ffn_kernel,
        name="ffn",
        grid=(m // FFN_TOKENS,),
        in_specs=[
            tile,
            _const_spec((1, D_MODEL)),
            pl.BlockSpec(memory_space=pl.ANY),
            pl.BlockSpec(memory_space=pl.ANY),
            _const_spec((1, D_MODEL)),
        ],
        out_specs=tile,
        out_shape=jax.ShapeDtypeStruct((m, D_MODEL), jnp.float32),
        scratch_shapes=[
            pltpu.VMEM((D_MODEL, 2 * D_FF), jnp.bfloat16),
            pltpu.VMEM((D_FF, D_MODEL + PAD_COLS), jnp.bfloat16),
            pltpu.VMEM((2, 2, D_MODEL, FFN_CHUNK), jnp.float32),
            pltpu.VMEM((D_FF, D_MODEL), jnp.float32),
            pltpu.SemaphoreType.DMA((4,)),
            pltpu.SemaphoreType.DMA((1,)),
            pltpu.VMEM((FFN_TOKENS, D_FF), jnp.bfloat16),
        ],
        compiler_params=pltpu.CompilerParams(
            dimension_semantics=("arbitrary",), vmem_limit_bytes=VMEM_LIMIT_BYTES),
    )(h2d, pre_g, w_gu, w_down, post_g)


def _mixer_kernel(sinks_ref, h_ref, cos_ref, sin_ref, pre_g_ref, w_in_ref, conv_w_ref,
                  conv_b_ref, w_gate_ref, b_a_ref, b_x_ref, lam_ref, w_proj_ref, post_g_ref, o_ref,
                  un_ref, unf_ref, unp_ref, xhist_ref, hstate_ref, q_ref, kall_ref, vall_ref, ylru_ref,
                  yattn_ref, sgl_ref, sga_ref, plru_ref, merged_ref):
    ts = MIX_TOKENS
    seg_rows = ts // SEG_LEN
    t_idx = pl.program_id(1)
    f32, bf16 = jnp.float32, jnp.bfloat16

    @pl.when(t_idx == 0)
    def _():
        xhist_ref[...] = jnp.zeros_like(xhist_ref)
        hstate_ref[...] = jnp.zeros_like(hstate_ref)
        kall_ref[0:BLOCK, :] = jnp.zeros((BLOCK, KV_SLAB_WIDTH), bf16)
        vall_ref[0:BLOCK, :] = jnp.zeros((BLOCK, KV_SLAB_WIDTH), bf16)

    un = _rms_norm(h_ref[...], pre_g_ref[...])
    un_ref[...] = un.astype(bf16)
    for lg in range(D_MODEL // LANES):
        unf_ref[lg] = un[:, lg * LANES:(lg + 1) * LANES]
    for lg in range(D_MODEL // LANES):
        pieces = [unf_ref.at[lg][pl.ds(SEG_LEN * SUBLANES * grp + i, SUBLANES, stride=SEG_LEN), :]
                  for i in range(SEG_LEN) for grp in range(seg_rows // SUBLANES)]
        unp_ref[:, lg * LANES:(lg + 1) * LANES] = jnp.concatenate(pieces, axis=0).astype(bf16)

    def proj(lo, src_ref=un_ref):
        return jnp.dot(src_ref[...], w_in_ref[:, lo:lo + GROUP_W], preferred_element_type=f32)

    lane = lax.broadcasted_iota(jnp.int32, (ts, LANES), 1)
    first_half = (lane % HEAD_DIM) < HALF
    first_head = lane < HEAD_DIM

    def rope(x, c, s):
        partner = jnp.where(first_half, pltpu.roll(x, LANES - HALF, axis=1),
                            pltpu.roll(x, HALF, axis=1))
        return x * c + partner * s

    cos = cos_ref[...]
    sin = sin_ref[...]
    q_scale = (HEAD_DIM ** -0.5) * LOG2E
    cos_q = cos * q_scale
    sin_q = sin * q_scale
    seg_row = lax.broadcasted_iota(jnp.int32, (seg_rows, GROUP_W), 0)

    def shift_rows(cur, first_row):
        return jnp.where(seg_row == 0, first_row, pltpu.roll(cur, 1, axis=0))

    kv_nat = proj(C_K)
    k_nat, v_nat = kv_nat[:, :KV_WIDTH], kv_nat[:, KV_WIDTH:]
    for jj in range(N_KV_HEADS // 2):
        kr = rope(k_nat[:, jj * LANES:(jj + 1) * LANES], cos, sin)
        vv = v_nat[:, jj * LANES:(jj + 1) * LANES]
        kr_swapped = pltpu.roll(kr, HEAD_DIM, axis=1)
        vv_swapped = pltpu.roll(vv, HEAD_DIM, axis=1)
        for half in range(2):
            j = 2 * jj + half
            for dst_ref, same, swapped, fill_low, fill_high in (
                    (kall_ref, kr, kr_swapped, 0.0, 0.0),
                    (vall_ref, vv, vv_swapped, (lane == SUM_LANE_LOW).astype(f32),
                     (lane == SUM_LANE_HIGH).astype(f32))):
                low, high = (same, swapped) if half == 0 else (swapped, same)
                dst_ref[BLOCK:BLOCK + ts, (2 * j) * LANES:(2 * j + 1) * LANES] = (
                    jnp.where(first_head, low, fill_low).astype(bf16))
                dst_ref[BLOCK:BLOCK + ts, (2 * j + 1) * LANES:(2 * j + 2) * LANES] = (
                    jnp.where(first_head, fill_high, high).astype(bf16))

    for g in range(GROUPS):
        qg = proj(C_Q + g * GROUP_W)
        for pp in range(GROUP_W // LANES):
            lo = g * GROUP_W + pp * LANES
            q_ref[:, lo:lo + LANES] = rope(qg[:, pp * LANES:(pp + 1) * LANES], cos_q, sin_q).astype(bf16)

    def lru_stages(g):
        cg = slice(g * GROUP_W, (g + 1) * GROUP_W)
        st = {}

        def conv_and_gates():
            xg = proj(C_X + g * GROUP_W, unp_ref)
            x_blk = [xg[i * seg_rows:(i + 1) * seg_rows, :] for i in range(SEG_LEN)]
            wrapped = {}
            for i_src in range(SEG_LEN - CONV_WIDTH + 1, SEG_LEN):
                k = i_src - (SEG_LEN - CONV_WIDTH + 1)
                wrapped[i_src] = shift_rows(x_blk[i_src], xhist_ref[k, :, cg])
                xhist_ref[k, :, cg] = x_blk[i_src][seg_rows - 1:seg_rows, :]
            xc_blk = []
            for i in range(SEG_LEN):
                acc = conv_b_ref[:, cg]
                for d in range(CONV_WIDTH):
                    src = x_blk[i - d] if i >= d else wrapped[i - d + SEG_LEN]
                    acc = acc + src * conv_w_ref[CONV_WIDTH - 1 - d:CONV_WIDTH - d, cg]
                xc_blk.append(acc)
            st["xc"] = jnp.concatenate(xc_blk, axis=0)
            xc_b = st["xc"].astype(bf16)
            st["gates"] = [
                jnp.dot(xc_b[:, k * MXU_DIM:(k + 1) * MXU_DIM], w_gate_ref[g * (GROUP_W // MXU_DIM) + k],
                        preferred_element_type=f32) for k in range(GROUP_W // MXU_DIM)]

        def decay_and_input():
            gates, xc = st["gates"], st["xc"]
            th_r = jnp.tanh(jnp.concatenate([ri[:, :MXU_DIM] for ri in gates], axis=1)
                            + 0.5 * b_a_ref[:, cg])
            th_i = jnp.tanh(jnp.concatenate([ri[:, MXU_DIM:] for ri in gates], axis=1)
                            + 0.5 * b_x_ref[:, cg])
            lam = lam_ref[:, cg]
            softplus_neg_lam = jnp.maximum(-lam, 0.0) + jnp.log1p(jnp.exp(-jnp.abs(lam)))
            k = (0.5 * LRU_C) * softplus_neg_lam
            neg_log_a = th_r * k + k
            a = jnp.exp2(neg_log_a * (-LOG2E))
            t = jnp.tanh(neg_log_a) * (a * a + 1.0)
            mult = jnp.where(t > 0.0, t * lax.rsqrt(t), 0.0)
            half_xc = 0.5 * xc
            st["a"], st["b"] = a, mult * (th_i * half_xc + half_xc)

        def local_scan():
            a, b = st["a"], st["b"]
            a_blk = [a[i * seg_rows:(i + 1) * seg_rows, :] for i in range(SEG_LEN)]
            b_blk = [b[i * seg_rows:(i + 1) * seg_rows, :] for i in range(SEG_LEN)]
            h_loc, a_cum = [b_blk[0]], [a_blk[0]]
            for i in range(1, SEG_LEN):
                h_loc.append(a_blk[i] * h_loc[-1] + b_blk[i])
                a_cum.append(a_blk[i] * a_cum[-1])
            st["h_loc"], st["a_cum"] = h_loc, a_cum

        def carry_and_gate():
            h_loc, a_cum = st["h_loc"], st["a_cum"]
            seg_a, seg_h = a_cum[-1], h_loc[-1]
            shift = 1
            while shift < SUBLANES:
                keep = (seg_row % SUBLANES) >= shift
                a_prev = jnp.where(keep, pltpu.roll(seg_a, shift, axis=0), 1.0)
                h_prev = jnp.where(keep, pltpu.roll(seg_h, shift, axis=0), 0.0)
                seg_h = seg_a * h_prev + seg_h
                seg_a = seg_a * a_prev
                shift *= 2
            carry = hstate_ref[:, cg]
            state_in = carry
            ends = []
            for v in range(seg_rows // SUBLANES):
                lo = v * SUBLANES
                hv = seg_a[lo:lo + SUBLANES, :] * carry + seg_h[lo:lo + SUBLANES, :]
                ends.append(hv)
                carry = hv[SUBLANES - 1:SUBLANES, :]
            hstate_ref[:, cg] = carry
            seg_in = shift_rows(jnp.concatenate(ends, axis=0), state_in)
            h_lru = jnp.concatenate([h_loc[i] + a_cum[i] * seg_in for i in range(SEG_LEN)], axis=0)
            ylru_ref[:, cg] = (h_lru * jax.nn.gelu(proj(C_GATE + g * GROUP_W, unp_ref))).astype(bf16)

        def merge_gates():
            sgl_ref[:, cg] = jnp.tanh(proj(C_GL + g * GROUP_W))
            sga_ref[:, cg] = jnp.tanh(proj(C_GA + g * GROUP_W))

        return [conv_and_gates, decay_and_input, local_scan, carry_and_gate, merge_gates]

    first_head2 = lax.broadcasted_iota(jnp.int32, (2 * BLOCK, LANES), 1) < HEAD_DIM
    qi = lax.broadcasted_iota(jnp.int32, (2 * BLOCK, 2 * BLOCK), 0) % BLOCK
    si = lax.broadcasted_iota(jnp.int32, (2 * BLOCK, 2 * BLOCK), 1)
    diff = BLOCK + qi - si
    band = (diff >= 0) & (diff < WINDOW)
    first_pair_rows = lax.broadcasted_iota(jnp.int32, (2 * BLOCK, 1), 0) < BLOCK

    def scores(j, qb):
        r0 = qb * BLOCK
        q2 = jnp.concatenate(
            [q_ref[r0:r0 + BLOCK, (2 * j + pp) * LANES:(2 * j + pp + 1) * LANES] for pp in range(2)],
            axis=0)
        mask = band & ((si >= BLOCK) | (t_idx * (ts // BLOCK) + qb > 0))
        halves = []
        for second in range(2):
            slab = slice((2 * j + second) * LANES, (2 * j + second + 1) * LANES)
            head = 4 * j + second
            sink = jnp.where(first_pair_rows, sinks_ref[head] * LOG2E, sinks_ref[head + 2] * LOG2E)
            s = lax.dot_general(q2, kall_ref[r0:r0 + 2 * BLOCK, slab], (((1,), (1,)), ((), ())),
                                preferred_element_type=f32)
            s = jnp.where(mask, s, MASK_VALUE)
            m = jnp.maximum(jnp.max(s, axis=1, keepdims=True), sink)
            halves.append((s, m, sink))
        return halves

    def weighted_values(j, qb, halves):
        r0 = qb * BLOCK
        parts = []
        for second, (s, m, sink) in enumerate(halves):
            slab = slice((2 * j + second) * LANES, (2 * j + second + 1) * LANES)
            e = jnp.exp2(s - m).astype(bf16)
            pv = jnp.dot(e, vall_ref[r0:r0 + 2 * BLOCK, slab], preferred_element_type=f32)
            sum_lane = SUM_LANE_HIGH if second else SUM_LANE_LOW
            denom = pv[:, sum_lane:sum_lane + 1] + jnp.exp2(sink - m)
            parts.append(pv * (1.0 / denom))
        out = jnp.where(first_head2, parts[0], parts[1]).astype(bf16)
        for pp in range(2):
            lo = (2 * j + pp) * LANES
            yattn_ref[r0:r0 + BLOCK, lo:lo + LANES] = out[pp * BLOCK:(pp + 1) * BLOCK, :]

    def attention_stages():
        blocks = [(j, qb) for j in range(N_KV_HEADS) for qb in range(ts // BLOCK)]
        pending = {}

        def stage(n):
            def run():
                if n < len(blocks):
                    pending[n] = scores(*blocks[n])
                if n >= 1:
                    weighted_values(*blocks[n - 1], pending.pop(n - 1))
            return run

        return [stage(n) for n in range(len(blocks) + 1)]

    def lru_projection(c):
        p_lru = jnp.dot(ylru_ref[...], w_proj_ref[:, P_LRU + c * GROUP_W:P_LRU + (c + 1) * GROUP_W],
                        preferred_element_type=f32)
        for half in range(GROUP_W // LANES):
            lg = c * (GROUP_W // LANES) + half
            for i in range(SEG_LEN):
                for grp in range(seg_rows // SUBLANES):
                    row = i * seg_rows + grp * SUBLANES
                    plru_ref.at[lg][pl.ds(SEG_LEN * SUBLANES * grp + i, SUBLANES, stride=SEG_LEN), :] = (
                        p_lru[row:row + SUBLANES, half * LANES:(half + 1) * LANES])

    lru_list = [stage for g in range(GROUPS) for stage in lru_stages(g)]
    lru_list += [lambda c=c: lru_projection(c) for c in range(GROUPS)]
    att_list = attention_stages()
    att_per_lru = -(-len(att_list) // len(lru_list))
    while lru_list or att_list:
        if lru_list:
            lru_list.pop(0)()
        for _ in range(att_per_lru):
            if att_list:
                att_list.pop(0)()

    kall_ref[0:BLOCK, :] = kall_ref[ts:ts + BLOCK, :]
    vall_ref[0:BLOCK, :] = vall_ref[ts:ts + BLOCK, :]

    for n in range(GROUPS):
        cn = slice(n * GROUP_W, (n + 1) * GROUP_W)
        p_attn = jnp.dot(yattn_ref[...], w_proj_ref[:, P_ATTN + n * GROUP_W:P_ATTN + (n + 1) * GROUP_W],
                         preferred_element_type=f32)
        p_lru = jnp.concatenate([plru_ref[n * (GROUP_W // LANES) + half] for half in range(GROUP_W // LANES)],
                                axis=1)
        merged_ref[:, cn] = (sgl_ref[:, cn] * p_lru + p_lru + sga_ref[:, cn] * p_attn + p_attn).astype(bf16)
    m_out = [jnp.dot(merged_ref[...], w_proj_ref[:, P_OUT + n * GROUP_W:P_OUT + (n + 1) * GROUP_W],
                     preferred_element_type=f32) for n in range(GROUPS)]
    mean_sq = sum(jnp.sum(mo * mo, axis=1, keepdims=True) for mo in m_out) * (1.0 / D_MODEL)
    inv_rms = lax.rsqrt(mean_sq + NORM_EPS)
    for n in range(GROUPS):
        cn = slice(n * GROUP_W, (n + 1) * GROUP_W)
        o_ref[:, cn] = h_ref[:, cn] + m_out[n] * inv_rms * post_g_ref[:, cn]


def _mixer(h3d, sinks, cos_t, sin_t, pre_g, w_in_p, conv_w, conv_b, w_gate, b_a, b_x, lam,
           w_proj, post_g):
    bsz, s, _ = h3d.shape
    ts = MIX_TOKENS
    tile = pl.BlockSpec((None, ts, D_MODEL), lambda b, t: (b, t, 0))
    rope_spec = pl.BlockSpec((ts, LANES), lambda b, t: (t, 0))
    return pl.pallas_call(
        _mixer_kernel,
        name="mixer",
        grid=(bsz, s // ts),
        in_specs=[
            pl.BlockSpec(memory_space=pltpu.SMEM),
            tile, rope_spec, rope_spec,
            _const_spec((1, D_MODEL)),
            _const_spec((D_MODEL, IN_WIDTH)),
            _const_spec((CONV_WIDTH, LRU_WIDTH)),
            _const_spec((1, LRU_WIDTH)),
            _const_spec((LRU_WIDTH // MXU_DIM, MXU_DIM, 2 * MXU_DIM)),
            _const_spec((1, LRU_WIDTH)),
            _const_spec((1, LRU_WIDTH)),
            _const_spec((1, LRU_WIDTH)),
            _const_spec((D_MODEL, 3 * D_MODEL + PAD_COLS)),
            _const_spec((1, D_MODEL)),
        ],
        out_specs=tile,
        out_shape=jax.ShapeDtypeStruct(h3d.shape, jnp.float32),
        scratch_shapes=[
            pltpu.VMEM((ts, D_MODEL), jnp.bfloat16),
            pltpu.VMEM((D_MODEL // LANES, ts, LANES), jnp.float32),
            pltpu.VMEM((ts, D_MODEL), jnp.bfloat16),
            pltpu.VMEM((CONV_WIDTH - 1, 1, LRU_WIDTH), jnp.float32),
            pltpu.VMEM((1, LRU_WIDTH), jnp.float32),
            pltpu.VMEM((ts, ATTN_WIDTH), jnp.bfloat16),
            pltpu.VMEM((BLOCK + ts, KV_SLAB_WIDTH), jnp.bfloat16),
            pltpu.VMEM((BLOCK + ts, KV_SLAB_WIDTH), jnp.bfloat16),
            pltpu.VMEM((ts, LRU_WIDTH), jnp.bfloat16),
            pltpu.VMEM((ts, ATTN_WIDTH), jnp.bfloat16),
            pltpu.VMEM((ts, D_MODEL), jnp.float32),
            pltpu.VMEM((ts, D_MODEL), jnp.float32),
            pltpu.VMEM((D_MODEL // LANES, ts, LANES), jnp.float32),
            pltpu.VMEM((ts, D_MODEL), jnp.bfloat16),
        ],
        compiler_params=pltpu.CompilerParams(
            dimension_semantics=("arbitrary", "arbitrary"), vmem_limit_bytes=VMEM_LIMIT_BYTES),
    )(sinks, h3d, cos_t, sin_t, pre_g, w_in_p, conv_w, conv_b, w_gate, b_a, b_x, lam,
      w_proj, post_g)


def _pad_cast(*ws):
    zeros = jnp.zeros((ws[0].shape[0], PAD_COLS), jnp.bfloat16)
    return jnp.concatenate([w.astype(jnp.bfloat16) for w in ws] + [zeros], axis=1)


def _scaled_w_in(w_in):
    scale = jnp.concatenate([jnp.ones((C_GL,), jnp.float32), jnp.full((IN_WIDTH - C_GL,), 0.5, jnp.float32)])
    return (w_in * scale[None, :]).astype(jnp.bfloat16)


def _gate_weights(w_a, w_x):
    per = MXU_DIM // LRU_BLOCK_W
    eye = jnp.eye(per, dtype=w_a.dtype)

    def expand(w):
        w4 = w.reshape(LRU_BLOCKS // per, per, LRU_BLOCK_W, LRU_BLOCK_W)
        return jnp.einsum("gncd,nm->gncmd", w4, eye).reshape(LRU_BLOCKS // per, MXU_DIM, MXU_DIM)

    return (0.5 * jnp.concatenate([expand(w_a), expand(w_x)], axis=2)).astype(jnp.bfloat16)


def _rope_tables(seq_len):
    inv_freq = ROPE_THETA ** (-np.arange(HALF, dtype=np.float64) / HALF)
    ang = np.arange(seq_len, dtype=np.float64)[:, None] * inv_freq[None, :]
    cos, sin = np.cos(ang), np.sin(ang)
    return (jnp.asarray(np.concatenate([cos, cos, cos, cos], axis=1), jnp.float32),
            jnp.asarray(np.concatenate([-sin, sin, -sin, sin], axis=1), jnp.float32))


def kernel(x, ffn1_pre_g, ffn1_w_gu, ffn1_w_down, ffn1_post_g, mix_pre_g, w_in, conv_w, conv_b, lru_w_a, lru_b_a, lru_w_x, lru_b_x, lru_lambda, attn_sinks, w_proj_lru, w_proj_attn, w_out, mix_post_g, ffn2_pre_g, ffn2_w_gu, ffn2_w_down, ffn2_post_g):
    bsz, s, d = x.shape
    depth = ffn1_pre_g.shape[0]
    cos_t, sin_t = _rope_tables(s)
    bf = jnp.bfloat16
    h = x
    for l in range(depth):
        h = _ffn(h.reshape(bsz * s, d), ffn1_pre_g[l][None], ffn1_w_gu[l], ffn1_w_down[l],
                 ffn1_post_g[l][None]).reshape(bsz, s, d)
        h = _mixer(h, attn_sinks[l], cos_t, sin_t, mix_pre_g[l][None], _scaled_w_in(w_in[l]),
                   conv_w[l], conv_b[l][None], _gate_weights(lru_w_a[l], lru_w_x[l]),
                   lru_b_a[l][None], lru_b_x[l][None], lru_lambda[l][None],
                   _pad_cast(0.5 * w_proj_lru[l], 0.5 * w_proj_attn[l], w_out[l]),
                   mix_post_g[l][None])
        h = _ffn(h.reshape(bsz * s, d), ffn2_pre_g[l][None], ffn2_w_gu[l], ffn2_w_down[l],
                 ffn2_post_g[l][None]).reshape(bsz, s, d)
    return h
```
